```python
import jax, jax.numpy as jnp
from jax import lax
import numpy as np

D_MODEL = 1024
BATCH = 8
SEQ = 2048
DEPTH = 4
DEC_BATCH = 32
DEC_SEQ = 1
PAST_LEN = 16384
PAGE_SIZE = 128

GDN_HEADS = D_MODEL // 128
GDN_DK = 128
GDN_DV = 128
GDN_DIM = GDN_HEADS * GDN_DK
CONV_W = 4
GDN_CHUNK = 64
GDN_IN = 4 * GDN_DIM + 2 * GDN_HEADS
MLA_HEADS = D_MODEL // 128
Q_LORA = D_MODEL // 2
KV_LORA = D_MODEL // 4
NOPE_DIM = 128
ROPE_DIM = 64
QK_DIM = NOPE_DIM + ROPE_DIM
V_DIM = 128
MLA_IN = Q_LORA + KV_LORA + ROPE_DIM
ROPE_THETA = 10000.0
ATTN_BLOCK = 128
SOFTMAX_SCALE = QK_DIM ** -0.5
PEER_HEADS = 8
N_KEYS = 128
N_EXPERTS = N_KEYS * N_KEYS
PEER_TOPK = 16
PEER_QDIM = 256
PEER_HALF = PEER_QDIM // 2
PEER_BLOCK = 128
EPS = 1e-6
N_GDN = (DEPTH + 1) // 2
N_MLA = DEPTH // 2

kernel_name = 'hybrid_gdn_mla_peer_step'


def rmsnorm(x, g):
    xf = x.astype(jnp.float32)
    y = xf * lax.rsqrt(jnp.mean(xf * xf, axis=-1, keepdims=True) + EPS)
    return (y * g.astype(jnp.float32)).astype(x.dtype)


def l2norm(x):
    xf = x.astype(jnp.float32)
    return xf * lax.rsqrt(jnp.sum(xf * xf, axis=-1, keepdims=True) + EPS)


def rope(x, pos):
    half = x.shape[-1] // 2
    freq = ROPE_THETA ** (-jnp.arange(half, dtype=jnp.float32) / half)
    ang = pos.astype(jnp.float32)[:, None] * freq[None, :]
    ang = ang.reshape(ang.shape[0], *([1] * (x.ndim - 3)), half)
    cos, sin = jnp.cos(ang).astype(x.dtype), jnp.sin(ang).astype(x.dtype)
    x1, x2 = x[..., :half], x[..., half:]
    return jnp.concatenate([x1 * cos - x2 * sin, x2 * cos + x1 * sin], axis=-1)


def causal_conv(x, buf, w):
    T = x.shape[1]
    xp = jnp.concatenate([buf.astype(x.dtype), x], axis=1)
    y = xp[:, 0:T] * w[0]
    for j in range(1, CONV_W):
        y = y + xp[:, j:j + T] * w[j]
    return y, xp[:, T:]


def gated_delta_chunked(q, k, v, beta, g, S0, chunk):
    B, T, H, _ = q.shape
    nc = T // chunk

    def to_chunks(t):
        return jnp.moveaxis(t.reshape(B, nc, chunk, H, *t.shape[3:]), 3, 1)

    q, k, v, beta, g = (to_chunks(t) for t in (q, k, v, beta, g))
    G = jnp.cumsum(g, axis=-1)
    incl = jnp.tril(jnp.ones((chunk, chunk), bool))
    strict = jnp.tril(jnp.ones((chunk, chunk), bool), -1)
    decay = jnp.exp(jnp.where(incl, G[..., :, None] - G[..., None, :], -jnp.inf))
    kk = jnp.einsum('bhnid,bhnjd->bhnij', k, k)
    a_mat = jnp.where(strict, beta[..., :, None] * kk * decay, 0.0) + jnp.eye(chunk, dtype=jnp.float32)

    def solve(rhs):
        return lax.linalg.triangular_solve(a_mat, rhs, left_side=True, lower=True, unit_diagonal=True)

    w_v = solve(v * beta[..., None])
    w_k = solve(k * (beta * jnp.exp(G))[..., None])
    qk = jnp.einsum('bhnid,bhnjd->bhnij', q, k) * decay
    q_dec = q * jnp.exp(G)[..., None]
    k_dec = k * jnp.exp(G[..., -1:] - G)[..., None]
    g_tot = jnp.exp(G[..., -1])
    xs = tuple(jnp.moveaxis(t, 2, 0) for t in (w_v, w_k, qk, q_dec, k_dec, g_tot))

    def step(S, inp):
        wv, wk, qkn, qd, kd, gt = inp
        u = wv - jnp.einsum('bhck,bhkv->bhcv', wk, S)
        o = jnp.einsum('bhck,bhkv->bhcv', qd, S) + jnp.einsum('bhij,bhjv->bhiv', qkn, u)
        S = S * gt[..., None, None] + jnp.einsum('bhck,bhcv->bhkv', kd, u)
        return S, o

    S, o = lax.scan(step, S0, xs)
    o = jnp.transpose(o, (1, 0, 3, 2, 4)).reshape(B, T, H, v.shape[-1])
    return o, S


def gdn_mixer(xn, conv_buf, S0, w_in, conv_w, a_log, dt_bias, g_out, w_out, chunk):
    B, T, _ = xn.shape
    proj = xn @ w_in
    qkv = proj[..., :3 * GDN_DIM]
    z = proj[..., 3 * GDN_DIM:4 * GDN_DIM].reshape(B, T, GDN_HEADS, GDN_DV)
    b = proj[..., 4 * GDN_DIM:4 * GDN_DIM + GDN_HEADS]
    a = proj[..., 4 * GDN_DIM + GDN_HEADS:]
    qkv, new_buf = causal_conv(qkv, conv_buf, conv_w)
    qkv = jax.nn.silu(qkv)
    q, k, v = jnp.split(qkv, 3, axis=-1)
    q = l2norm(q.reshape(B, T, GDN_HEADS, GDN_DK)) * (GDN_DK ** -0.5)
    k = l2norm(k.reshape(B, T, GDN_HEADS, GDN_DK))
    v = v.reshape(B, T, GDN_HEADS, GDN_DV).astype(jnp.float32)
    beta = jax.nn.sigmoid(b.astype(jnp.float32))
    g = -jnp.exp(a_log.astype(jnp.float32)) * jax.nn.softplus(a.astype(jnp.float32) + dt_bias.astype(jnp.float32))
    o, S = gated_delta_chunked(q, k, v, beta, g, S0.astype(jnp.float32), chunk)
    o = (rmsnorm(o, g_out) * jax.nn.silu(z.astype(jnp.float32))).astype(xn.dtype)
    y = o.reshape(B, T, GDN_DIM) @ w_out
    return y, S.astype(xn.dtype), new_buf


def latent_attention(q_lat, q_rope, c, kr, ksc, q_pos, k_pos):
    B, Tq, H, R = q_lat.shape
    blk = ATTN_BLOCK if Tq % ATTN_BLOCK == 0 else Tq
    nb = Tq // blk
    ks = jnp.transpose(ksc, (0, 2, 1))[:, :, None, :]

    def block(args):
        ql, qr, qp = args
        s = (jnp.einsum('bqhr,bkr->bhqk', ql, c, preferred_element_type=jnp.float32)
             + jnp.einsum('bqhp,bkp->bhqk', qr, kr, preferred_element_type=jnp.float32))
        s = s * ks.astype(jnp.float32) * SOFTMAX_SCALE
        s = jnp.where(k_pos[None, :] <= qp[:, None], s, -jnp.inf)
        p = jax.nn.softmax(s, axis=-1)
        return jnp.einsum('bhqk,bkr->bqhr', p.astype(c.dtype), c)

    qlb = jnp.moveaxis(q_lat.reshape(B, nb, blk, H, R), 1, 0)
    qrb = jnp.moveaxis(q_rope.reshape(B, nb, blk, H, q_rope.shape[-1]), 1, 0)
    o = lax.map(block, (qlb, qrb, q_pos.reshape(nb, blk)))
    return jnp.moveaxis(o, 0, 1).reshape(B, Tq, H, R)


def mla_mixer(xn, past_c, past_kr, past_ks, w_in, g_cq, g_ckv, w_uq, g_qn, g_kn, w_uk, w_uv, w_out):
    B, T, _ = xn.shape
    past = 0 if past_c is None else past_c.shape[1]
    q_pos = past + jnp.arange(T)
    proj = xn @ w_in
    cq = rmsnorm(proj[..., :Q_LORA], g_cq)
    ckv = rmsnorm(proj[..., Q_LORA:Q_LORA + KV_LORA], g_ckv)
    kr = rope(proj[..., Q_LORA + KV_LORA:], q_pos)
    q = (cq @ w_uq).reshape(B, T, MLA_HEADS, QK_DIM)
    q = jnp.concatenate([q[..., :NOPE_DIM], rope(q[..., NOPE_DIM:], q_pos)], axis=-1)
    q = rmsnorm(q, g_qn) * g_kn
    k_nope = jnp.einsum('btr,rhd->bthd', ckv, w_uk).astype(jnp.float32)
    krf = kr.astype(jnp.float32)
    ksq = jnp.sum(k_nope * k_nope, -1) + jnp.sum(krf * krf, -1)[..., None]
    ksc = lax.rsqrt(ksq / QK_DIM + EPS).astype(xn.dtype)
    q_lat = jnp.einsum('bthd,rhd->bthr', q[..., :NOPE_DIM], w_uk)
    q_rope = q[..., NOPE_DIM:]
    if past_c is None:
        c_all, kr_all, ks_all = ckv, kr, ksc
    else:
        c_all = jnp.concatenate([past_c.astype(ckv.dtype), ckv], axis=1)
        kr_all = jnp.concatenate([past_kr.astype(kr.dtype), kr], axis=1)
        ks_all = jnp.concatenate([past_ks.astype(ksc.dtype), ksc], axis=1)
    k_pos = jnp.arange(past + T)
    o_lat = latent_attention(q_lat, q_rope, c_all, kr_all, ks_all, q_pos, k_pos)
    o = jnp.einsum('bthr,rhd->bthd', o_lat, w_uv).reshape(B, T, MLA_HEADS * V_DIM)
    return o @ w_out, ckv, kr, ksc


def peer_ffn(xn, w_q, sub_keys, u_tab, v_tab):
    lead = xn.shape[:-1]
    x2 = xn.reshape(-1, xn.shape[-1])
    T = x2.shape[0]
    q = (x2 @ w_q).reshape(T, PEER_HEADS, 2, PEER_HALF)
    s = jnp.einsum('thpd,hpnd->thpn', q, sub_keys, preferred_element_type=jnp.float32)
    sv, si = lax.top_k(s, PEER_TOPK)
    cand = (sv[:, :, 0, :, None] + sv[:, :, 1, None, :]).reshape(T, PEER_HEADS, PEER_TOPK * PEER_TOPK)
    cidx = (si[:, :, 0, :, None] * N_KEYS + si[:, :, 1, None, :]).reshape(T, PEER_HEADS, PEER_TOPK * PEER_TOPK)
    top_s, top_i = lax.top_k(cand, PEER_TOPK)
    idx = jnp.take_along_axis(cidx, top_i, axis=-1).reshape(T, PEER_HEADS * PEER_TOPK)
    gate = jax.nn.softmax(top_s, axis=-1).reshape(T, PEER_HEADS * PEER_TOPK)
    blk = PEER_BLOCK if T % PEER_BLOCK == 0 else T
    nb = T // blk

    def block(args):
        xb, ib, gb = args
        h = jnp.einsum('td,ted->te', xb, u_tab[ib], preferred_element_type=jnp.float32)
        act = jax.nn.gelu(h, approximate=False) * gb
        return jnp.einsum('te,ted->td', act.astype(xb.dtype), v_tab[ib])

    out = lax.map(block, (x2.reshape(nb, blk, -1), idx.reshape(nb, blk, -1), gate.reshape(nb, blk, -1)))
    return out.reshape(*lead, x2.shape[-1])


def setup_inputs(seed: int = 0) -> dict:
    key = jax.random.key(seed)
    ks = jax.random.split(key, 32)
    f32 = jnp.float32

    def nrm(i, shape, scale):
        return jax.random.normal(ks[i], shape, f32) * scale

    def gain(i, shape):
        return 1.0 + 0.01 * jax.random.normal(ks[i], shape, f32)

    n_pages = PAST_LEN // PAGE_SIZE
    n_used = DEC_BATCH * n_pages
    n_pool = n_used + max(1, n_used // 4)
    page_table = jax.random.permutation(ks[5], n_pool)[:n_used].reshape(DEC_BATCH, n_pages).astype(jnp.int32)
    dt = jax.random.uniform(ks[12], (N_GDN, GDN_HEADS), f32, 1e-3, 1e-1)
    return {
        'x_prompt': nrm(0, (BATCH, SEQ, D_MODEL), 1.0),
        'x_sample': nrm(1, (DEC_BATCH, DEC_SEQ, D_MODEL), 1.0),
        'cache_latent': nrm(2, (N_MLA, n_pool, PAGE_SIZE, KV_LORA), 1.0),
        'cache_krope': nrm(3, (N_MLA, n_pool, PAGE_SIZE, ROPE_DIM), 1.0),
        'cache_kscale': jax.random.uniform(ks[4], (N_MLA, n_pool, PAGE_SIZE, MLA_HEADS), f32, 0.8, 1.2),
        'page_table': page_table,
        'state_ssm': nrm(6, (N_GDN, DEC_BATCH, GDN_HEADS, GDN_DK, GDN_DV), 0.1),
        'state_conv': nrm(7, (N_GDN, DEC_BATCH, CONV_W - 1, 3 * GDN_DIM), 1.0),
        'norm_mix': gain(8, (DEPTH, D_MODEL)),
        'norm_ffn': gain(9, (DEPTH, D_MODEL)),
        'gdn_w_in': nrm(10, (N_GDN, D_MODEL, GDN_IN), D_MODEL ** -0.5),
        'gdn_conv_w': nrm(11, (N_GDN, CONV_W, 3 * GDN_DIM), CONV_W ** -0.5),
        'gdn_a_log': jnp.log(jax.random.uniform(ks[13], (N_GDN, GDN_HEADS), f32, 1.0, 16.0)),
        'gdn_dt_bias': jnp.log(jnp.expm1(dt)),
        'gdn_out_norm': gain(14, (N_GDN, GDN_DV)),
        'gdn_w_out': nrm(15, (N_GDN, GDN_DIM, D_MODEL), GDN_DIM ** -0.5),
        'mla_w_in': nrm(16, (N_MLA, D_MODEL, MLA_IN), D_MODEL ** -0.5),
        'mla_cq_norm': gain(17, (N_MLA, Q_LORA)),
        'mla_ckv_norm': gain(18, (N_MLA, KV_LORA)),
        'mla_w_uq': nrm(19, (N_MLA, Q_LORA, MLA_HEADS * QK_DIM), Q_LORA ** -0.5),
        'mla_q_norm': gain(20, (N_MLA, QK_DIM)),
        'mla_k_norm': gain(21, (N_MLA, QK_DIM)),
        'mla_w_uk': nrm(22, (N_MLA, KV_LORA, MLA_HEADS, NOPE_DIM), KV_LORA ** -0.5),
        'mla_w_uv': nrm(23, (N_MLA, KV_LORA, MLA_HEADS, V_DIM), KV_LORA ** -0.5),
        'mla_w_out': nrm(24, (N_MLA, MLA_HEADS * V_DIM, D_MODEL), (MLA_HEADS * V_DIM) ** -0.5),
        'peer_w_q': nrm(25, (DEPTH, D_MODEL, PEER_HEADS * PEER_QDIM), D_MODEL ** -0.5),
        'peer_sub_keys': nrm(26, (DEPTH, PEER_HEADS, 2, N_KEYS, PEER_HALF), PEER_HALF ** -0.5),
        'peer_u': nrm(27, (DEPTH, N_EXPERTS, D_MODEL), D_MODEL ** -0.5),
        'peer_v': nrm(28, (DEPTH, N_EXPERTS, D_MODEL), (PEER_HEADS * PEER_TOPK) ** -0.5),
    }


def reference(x_prompt, x_sample, cache_latent, cache_krope, cache_kscale, page_table, state_ssm, state_conv,
              norm_mix, norm_ffn, gdn_w_in, gdn_conv_w, gdn_a_log, gdn_dt_bias, gdn_out_norm, gdn_w_out,
              mla_w_in, mla_cq_norm, mla_ckv_norm, mla_w_uq, mla_q_norm, mla_k_norm, mla_w_uk, mla_w_uv, mla_w_out,
              peer_w_q, peer_sub_keys, peer_u, peer_v):
    xp, xs = x_prompt, x_sample
    bp, bs = xp.shape[0], xs.shape[0]
    past_len = page_table.shape[1] * cache_latent.shape[2]
    lat_p, kr_p, ksc_p, ssm_p, conv_p = [], [], [], [], []
    lat_s, kr_s, ksc_s, ssm_s, conv_s = [], [], [], [], []
    for i in range(DEPTH):
        j = i // 2
        hp = rmsnorm(xp, norm_mix[i])
        hs = rmsnorm(xs, norm_mix[i])
        if i % 2 == 0:
            gw = (gdn_w_in[j], gdn_conv_w[j], gdn_a_log[j], gdn_dt_bias[j], gdn_out_norm[j], gdn_w_out[j])
            buf0 = jnp.zeros((bp, CONV_W - 1, 3 * GDN_DIM), xp.dtype)
            s0 = jnp.zeros((bp, GDN_HEADS, GDN_DK, GDN_DV), xp.dtype)
            yp, sp, cp = gdn_mixer(hp, buf0, s0, *gw, chunk=GDN_CHUNK)
            ys, ss, cs = gdn_mixer(hs, state_conv[j], state_ssm[j], *gw, chunk=xs.shape[1])
            ssm_p.append(sp); conv_p.append(cp); ssm_s.append(ss); conv_s.append(cs)
        else:
            mw = (mla_w_in[j], mla_cq_norm[j], mla_ckv_norm[j], mla_w_uq[j], mla_q_norm[j], mla_k_norm[j],
                  mla_w_uk[j], mla_w_uv[j], mla_w_out[j])
            yp, lp, rp, kp = mla_mixer(hp, None, None, None, *mw)
            past_c = cache_latent[j][page_table].reshape(bs, past_len, KV_LORA)
            past_r = cache_krope[j][page_table].reshape(bs, past_len, ROPE_DIM)
            past_k = cache_kscale[j][page_table].reshape(bs, past_len, MLA_HEADS)
            ys, ls, rs, kss = mla_mixer(hs, past_c, past_r, past_k, *mw)
            lat_p.append(lp); kr_p.append(rp); ksc_p.append(kp)
            lat_s.append(ls); kr_s.append(rs); ksc_s.append(kss)
        xp = xp + yp
        xs = xs + ys
        pw = (peer_w_q[i], peer_sub_keys[i], peer_u[i], peer_v[i])
        xp = xp + peer_ffn(rmsnorm(xp, norm_ffn[i]), *pw)
        xs = xs + peer_ffn(rmsnorm(xs, norm_ffn[i]), *pw)
    return (xp, xs,
            jnp.stack(lat_p), jnp.stack(kr_p), jnp.stack(ksc_p), jnp.stack(ssm_p), jnp.stack(conv_p),
            jnp.stack(lat_s), jnp.stack(kr_s), jnp.stack(ksc_s), jnp.stack(ssm_s), jnp.stack(conv_s))
```

```python
import functools
import math

import numpy as np
import jax
import jax.numpy as jnp
from jax import lax
from jax.experimental import pallas as pl
from jax.experimental.pallas import tpu as pltpu

F32 = jnp.float32
BF16 = jnp.bfloat16
HI = lax.Precision.HIGHEST

EPS = 1e-6
ROPE_THETA = 10000.0
GDN_CHUNK = 64
CONV_W = 4
PEER_TOPK = 16
LANES = 128
VMEM_LIMIT = 52 * 1024 * 1024
SAMPLE_PAD = 128

NT_DIMS = (((1,), (1,)), ((), ()))
TN_DIMS = (((0,), (0,)), ((), ()))


def _params(sem):
    return pltpu.CompilerParams(dimension_semantics=sem, vmem_limit_bytes=VMEM_LIMIT)


def _iota(shape, dim):
    return lax.broadcasted_iota(jnp.int32, shape, dim)


def _silu(x):
    return x * jax.nn.sigmoid(x)


def _softplus(x):
    return jnp.maximum(x, 0.0) + jnp.log1p(jnp.exp(-jnp.abs(x)))


def _dot(a, b, precision=None):
    return jnp.dot(a, b, preferred_element_type=F32, precision=precision)


def _dot_nt(a, b, precision=None):
    return lax.dot_general(a, b, NT_DIMS, preferred_element_type=F32, precision=precision)


def _mm_kernel(*refs, has_gain, has_resid, emit_xn):
    it = iter(refs)
    x_ref, w_ref = next(it), next(it)
    g_ref = next(it) if has_gain else None
    r_ref = next(it) if has_resid else None
    o_ref = next(it)
    xn_ref = next(it) if emit_xn else None
    xb_scr = next(it)

    @pl.when(pl.program_id(1) == 0)
    def _():
        x = x_ref[...].astype(F32)
        if has_gain:
            x = x * lax.rsqrt(jnp.mean(x * x, axis=-1, keepdims=True) + EPS) * g_ref[...]
        xb = x.astype(BF16)
        xb_scr[...] = xb
        if emit_xn:
            xn_ref[...] = xb

    y = _dot(xb_scr[...], w_ref[...])
    if has_resid:
        y = y + r_ref[...]
    o_ref[...] = y.astype(o_ref.dtype)


def _pick_tile(n, cap):
    if n <= cap:
        return n
    best = LANES
    for t in range(LANES, cap + 1, LANES):
        if n % t == 0:
            best = t
    return best


def _mm(x, w, gain=None, resid=None, emit_xn=False, tm_cap=1024, tn_cap=1024):
    m, k = x.shape
    n = w.shape[1]
    tm = _pick_tile(m, tm_cap)
    tn = _pick_tile(n, tn_cap)
    assert m % tm == 0 and n % tn == 0
    in_specs = [pl.BlockSpec((tm, k), lambda i, j: (i, 0)),
                pl.BlockSpec((k, tn), lambda i, j: (0, j))]
    args = [x, w]
    if gain is not None:
        in_specs.append(pl.BlockSpec((1, k), lambda i, j: (0, 0)))
        args.append(gain.reshape(1, k).astype(F32))
    if resid is not None:
        in_specs.append(pl.BlockSpec((tm, tn), lambda i, j: (i, j)))
        args.append(resid)
    out_shape = [jax.ShapeDtypeStruct((m, n), F32)]
    out_specs = [pl.BlockSpec((tm, tn), lambda i, j: (i, j))]
    if emit_xn:
        out_shape.append(jax.ShapeDtypeStruct((m, k), BF16))
        out_specs.append(pl.BlockSpec((tm, k), lambda i, j: (i, 0)))
    res = pl.pallas_call(
        functools.partial(_mm_kernel, has_gain=gain is not None, has_resid=resid is not None, emit_xn=emit_xn),
        grid=(m // tm, n // tn),
        in_specs=in_specs,
        out_specs=out_specs,
        out_shape=out_shape,
        scratch_shapes=[pltpu.VMEM((tm, k), BF16)],
        compiler_params=_params(("parallel", "arbitrary")),
    )(*args)
    return res if emit_xn else res[0]


def _gdn_kernel(q_ref, k_ref, v_ref, z_ref, rows_ref, alog_ref, dt_ref, cwq_ref, cwk_ref, cwv_ref,
                gout_ref, bq_ref, bk_ref, bv_ref, s0_ref, o_ref, sfin_ref,
                xq_scr, xk_scr, xv_scr, s_scr, *, tb, chunk, valid_len):
    t = pl.program_id(2)
    dk = q_ref.shape[-1]
    hist = CONV_W - 1
    base = 8 - hist

    @pl.when(t == 0)
    def _():
        s_scr[...] = s0_ref[0, 0]
        xq_scr[base:8, :] = bq_ref[0]
        xk_scr[base:8, :] = bk_ref[0]
        xv_scr[base:8, :] = bv_ref[0]

    def conv_silu(x_ref, w_ref, scr):
        scr[8:8 + tb, :] = x_ref[0]
        y = w_ref[0:1, :] * scr[base:base + tb, :]
        for j in range(1, CONV_W):
            y = y + w_ref[j:j + 1, :] * scr[base + j:base + j + tb, :]
        scr[base:8, :] = scr[tb + base:tb + 8, :]
        return _silu(y)

    def l2n(x):
        return x * lax.rsqrt(jnp.sum(x * x, axis=-1, keepdims=True) + EPS)

    q = l2n(conv_silu(q_ref, cwq_ref, xq_scr)) * (dk ** -0.5)
    k = l2n(conv_silu(k_ref, cwk_ref, xk_scr))
    v = conv_silu(v_ref, cwv_ref, xv_scr)

    neg_a = -jnp.exp(alog_ref[0])
    dtb = dt_ref[0]
    ri = _iota((chunk, chunk), 0)
    ci = _iota((chunk, chunk), 1)
    eye = (ri == ci).astype(F32)
    incl = ri >= ci
    strict = ri > ci
    tril_f = incl.astype(F32)
    triu_f = (ri <= ci).astype(F32)
    rid8 = _iota((8, chunk), 0)
    n_double = max(int(math.ceil(math.log2(chunk))) - 1, 0)

    for c in range(tb // chunk):
        r0 = c * chunk
        raw = rows_ref[0, 0, c]
        m8 = jnp.where(rid8 == 0, jax.nn.sigmoid(raw),
                       jnp.where(rid8 == 1, neg_a * _softplus(raw + dtb), 0.0))
        if valid_len is not None:
            pos = t * tb + r0 + _iota((8, chunk), 1)
            m8 = jnp.where(pos < valid_len, m8, 0.0)
        g_row = _dot(m8, triu_f, HI)[1:2, :]
        cols = _dot_nt(eye, m8, HI)
        beta_col = cols[:, 0:1]
        g_col = _dot_nt(tril_f, m8, HI)[:, 1:2]
        decay = jnp.exp(jnp.where(incl, g_col - g_row, -jnp.inf))

        qc, kc, vc = q[r0:r0 + chunk], k[r0:r0 + chunk], v[r0:r0 + chunk]
        kb = kc.astype(BF16)
        kk = _dot_nt(kb, kb)
        pw = -jnp.where(strict, beta_col * kk * decay, 0.0)
        tinv = eye + pw
        for _ in range(n_double):
            pw = _dot(pw, pw, HI)
            tinv = tinv + _dot(tinv, pw, HI)
        e_g = jnp.exp(g_col)
        rhs = jnp.concatenate([vc * beta_col, kc * (beta_col * e_g)], axis=1)
        w = _dot(tinv, rhs, HI)
        w_v, w_k = w[:, :dk], w[:, dk:]
        qk = _dot_nt(qc.astype(BF16), kb) * decay
        g_last = g_col[chunk - 1:chunk, :]
        q_dec = qc * e_g
        k_dec = kc * jnp.exp(g_last - g_col)

        s = s_scr[...]
        sb = s.astype(BF16)
        u = w_v - _dot(w_k.astype(BF16), sb)
        ub = u.astype(BF16)
        o = _dot(q_dec.astype(BF16), sb) + _dot(qk.astype(BF16), ub)
        s_scr[...] = s * jnp.exp(g_last) + lax.dot_general(k_dec.astype(BF16), ub, TN_DIMS,
                                                           preferred_element_type=F32)
        zc = z_ref[0, r0:r0 + chunk, :]
        on = o * lax.rsqrt(jnp.mean(o * o, axis=-1, keepdims=True) + EPS) * gout_ref[...]
        o_ref[0, r0:r0 + chunk, :] = on * _silu(zc)

    @pl.when(t == pl.num_programs(2) - 1)
    def _():
        sfin_ref[0, 0] = s_scr[...]


def _gdn_mixer(proj, rows, a_log, dt_bias, conv_w, g_out, buf0, s0, valid_len, tb):
    b, t, _ = proj.shape
    nh = s0.shape[1]
    dk = s0.shape[2]
    chunk = rows.shape[-1]
    dim = nh * dk
    col = lambda off: pl.BlockSpec((1, tb, dk), lambda bi, h, ti, off=off: (bi, ti, off + h))
    cw = lambda off: pl.BlockSpec((CONV_W, dk), lambda bi, h, ti, off=off: (0, off + h))
    bf = lambda off: pl.BlockSpec((1, CONV_W - 1, dk), lambda bi, h, ti, off=off: (bi, 0, off + h))
    hscal = pl.BlockSpec((1, 1, 1), lambda bi, h, ti: (h, 0, 0))
    o, sfin = pl.pallas_call(
        functools.partial(_gdn_kernel, tb=tb, chunk=chunk, valid_len=valid_len),
        grid=(b, nh, t // tb),
        in_specs=[col(0), col(nh), col(2 * nh), col(3 * nh),
                  pl.BlockSpec((1, 1, tb // chunk, 8, chunk), lambda bi, h, ti: (bi, h, ti, 0, 0)),
                  hscal, hscal, cw(0), cw(nh), cw(2 * nh),
                  pl.BlockSpec((1, dk), lambda bi, h, ti: (0, 0)),
                  bf(0), bf(nh), bf(2 * nh),
                  pl.BlockSpec((1, 1, dk, dk), lambda bi, h, ti: (bi, h, 0, 0))],
        out_specs=[pl.BlockSpec((1, tb, dk), lambda bi, h, ti: (bi, ti, h)),
                   pl.BlockSpec((1, 1, dk, dk), lambda bi, h, ti: (bi, h, 0, 0))],
        out_shape=[jax.ShapeDtypeStruct((b, t, dim), F32),
                   jax.ShapeDtypeStruct((b, nh, dk, dk), F32)],
        scratch_shapes=[pltpu.VMEM((tb + 8, dk), F32)] * 3 + [pltpu.VMEM((dk, dk), F32)],
        compiler_params=_params(("parallel", "parallel", "arbitrary")),
    )(proj, proj, proj, proj, rows, a_log.reshape(nh, 1, 1), dt_bias.reshape(nh, 1, 1),
      conv_w, conv_w, conv_w, g_out.reshape(1, dk), buf0, buf0, buf0, s0)
    return o, sfin


def _gdn_rows(ba, nh, chunk):
    b, t, _ = ba.shape
    x = jnp.stack([ba[..., :nh], ba[..., nh:2 * nh]], axis=-1)
    x = x.reshape(b, t // chunk, chunk, nh, 2).transpose(0, 3, 1, 4, 2)
    return jnp.pad(x, ((0, 0),) * 3 + ((0, 6), (0, 0)))


def _rope_block(x, cos_t, sin_a, sin_b):
    q4 = x.shape[-1] // 4
    return x * cos_t + pltpu.roll(x, 3 * q4, 1) * sin_a + pltpu.roll(x, q4, 1) * sin_b


def _mla_prep_kernel(p_ref, gcq_ref, gckv_ref, wuq_ref, gqn_ref, gqr_ref, wuk_ref, seg_ref, segt_ref,
                     seg8_ref, cos_ref, sina_ref, sinb_ref,
                     lat_ref, kr_ref, ksc_ref, qcat_ref, kcat_ref, kst_ref,
                     *, q_lora, kv_lora, nh, nope, rope_dim, scale):
    qk_dim = nope + rope_dim
    hn = nh * nope
    x = p_ref[...]
    cq = x[:, :q_lora]
    cq = cq * lax.rsqrt(jnp.mean(cq * cq, axis=-1, keepdims=True) + EPS) * gcq_ref[...]
    ckv = x[:, q_lora:q_lora + kv_lora]
    ckv = ckv * lax.rsqrt(jnp.mean(ckv * ckv, axis=-1, keepdims=True) + EPS) * gckv_ref[...]
    cos_t, sin_a, sin_b = cos_ref[...], sina_ref[...], sinb_ref[...]
    krb = _rope_block(x[:, q_lora + kv_lora:], cos_t, sin_a, sin_b)
    lat_ref[...] = ckv
    kr_ref[...] = krb[:, :rope_dim]
    ckv_b = ckv.astype(BF16)
    kcat_ref[:, :kv_lora] = ckv_b
    kcat_ref[:, kv_lora:] = krb.astype(BF16)

    qf = _dot(cq.astype(BF16), wuq_ref[...])
    qn = qf[:, :hn]
    qr = jnp.concatenate([_rope_block(qf[:, hn + h * nope:hn + (h + 1) * nope], cos_t, sin_a, sin_b)
                          for h in range(nh)], axis=1)
    ssq = _dot(qn * qn + qr * qr, seg_ref[...], HI)
    inv = lax.rsqrt(ssq * (1.0 / qk_dim) + EPS)
    inv_x = _dot(inv, segt_ref[...], HI)
    qn = qn * inv_x * gqn_ref[...]
    qr = qr * inv_x * gqr_ref[...]

    k_nope = _dot(ckv_b, wuk_ref[...])
    k2 = k_nope * k_nope
    kr2 = krb * krb
    ksq = _dot(k2, seg_ref[...], HI) + jnp.sum(kr2, axis=-1, keepdims=True)
    ksc = lax.rsqrt(ksq * (1.0 / qk_dim) + EPS)
    ksc_ref[...] = ksc[:, :nh]
    ksq_t = _dot_nt(seg8_ref[...], k2, HI) + _dot_nt(jnp.ones((8, kr2.shape[1]), F32), kr2, HI)
    kst_ref[...] = lax.rsqrt(ksq_t * (1.0 / qk_dim) + EPS) * scale

    for h in range(nh):
        sl = slice(h * nope, (h + 1) * nope)
        q_lat = _dot_nt(qn[:, sl].astype(BF16), wuk_ref[:, sl])
        qcat_ref[h, :, :kv_lora] = q_lat.astype(BF16)
        qcat_ref[h, :, kv_lora:] = qr[:, sl].astype(BF16)


def _mla_prep(proj, w, cos_t, sin_a, sin_b, tm):
    m = proj.shape[0]
    nh, nope, rope_dim, q_lora, kv_lora = w["nh"], w["nope"], w["rope"], w["q_lora"], w["kv_lora"]
    hn = nh * nope
    kc = kv_lora + LANES
    full = lambda a: pl.BlockSpec(a.shape, lambda i: (0,) * a.ndim)
    rows = lambda n: pl.BlockSpec((tm, n), lambda i: (i, 0))
    consts = [w["gcq"], w["gckv"], w["wuq"], w["gqn"], w["gqr"], w["wuk"], w["seg"], w["segt"], w["seg8"]]
    return pl.pallas_call(
        functools.partial(_mla_prep_kernel, q_lora=q_lora, kv_lora=kv_lora, nh=nh, nope=nope,
                          rope_dim=rope_dim, scale=float((nope + rope_dim) ** -0.5)),
        grid=(m // tm,),
        in_specs=[rows(proj.shape[1])] + [full(a) for a in consts] + [rows(LANES)] * 3,
        out_specs=[rows(kv_lora), rows(rope_dim), rows(nh),
                   pl.BlockSpec((nh, tm, kc), lambda i: (0, i, 0)), rows(kc),
                   pl.BlockSpec((8, tm), lambda i: (0, i))],
        out_shape=[jax.ShapeDtypeStruct((m, kv_lora), F32), jax.ShapeDtypeStruct((m, rope_dim), F32),
                   jax.ShapeDtypeStruct((m, nh), F32), jax.ShapeDtypeStruct((nh, m, kc), BF16),
                   jax.ShapeDtypeStruct((m, kc), BF16), jax.ShapeDtypeStruct((8, m), F32)],
        compiler_params=_params(("parallel",)),
    )(proj, *consts, cos_t, sin_a, sin_b)


def _attn_kernel(q_ref, k_ref, ks_ref, wuv_ref, o_ref, m_scr, l_scr, acc_scr, *, nh, kv_lora, vdim, tq, tk):
    qi, kj = pl.program_id(1), pl.program_id(2)

    @pl.when(kj == 0)
    def _():
        m_scr[...] = jnp.full(m_scr.shape, -jnp.inf, F32)
        l_scr[...] = jnp.zeros(l_scr.shape, F32)
        acc_scr[...] = jnp.zeros(acc_scr.shape, F32)

    @pl.when(kj <= qi)
    def _():
        kblk = k_ref[0]
        vblk = kblk[:, :kv_lora]
        causal = (kj * tk + _iota((tq, tk), 1)) <= (qi * tq + _iota((tq, tk), 0))
        for h in range(nh):
            s = _dot_nt(q_ref[h, 0], kblk) * ks_ref[0, h:h + 1, :]
            s = jnp.where(causal, s, -jnp.inf)
            m_old = m_scr[h]
            m_new = jnp.maximum(m_old, jnp.max(s, axis=-1, keepdims=True))
            alpha = jnp.exp(m_old - m_new)
            p = jnp.exp(s - m_new)
            l_scr[h] = alpha * l_scr[h] + jnp.sum(p, axis=-1, keepdims=True)
            acc_scr[h] = alpha * acc_scr[h] + _dot(p.astype(BF16), vblk)
            m_scr[h] = m_new

    @pl.when(kj == qi)
    def _():
        for h in range(nh):
            o_lat = acc_scr[h] / l_scr[h]
            o_ref[0, :, h * vdim:(h + 1) * vdim] = _dot(o_lat.astype(BF16), wuv_ref[:, h * vdim:(h + 1) * vdim])


def _mla_attention(qcat, kcat, kst, wuv, nh, kv_lora, vdim, tq):
    _, b, t, kc = qcat.shape
    tk = tq
    return pl.pallas_call(
        functools.partial(_attn_kernel, nh=nh, kv_lora=kv_lora, vdim=vdim, tq=tq, tk=tk),
        grid=(b, t // tq, t // tk),
        in_specs=[pl.BlockSpec((nh, 1, tq, kc), lambda bi, qi, kj: (0, bi, qi, 0)),
                  pl.BlockSpec((1, tk, kc), lambda bi, qi, kj: (bi, jnp.minimum(kj, qi), 0)),
                  pl.BlockSpec((1, 8, tk), lambda bi, qi, kj: (bi, 0, jnp.minimum(kj, qi))),
                  pl.BlockSpec(wuv.shape, lambda bi, qi, kj: (0, 0))],
        out_specs=pl.BlockSpec((1, tq, nh * vdim), lambda bi, qi, kj: (bi, qi, 0)),
        out_shape=jax.ShapeDtypeStruct((b, t, nh * vdim), F32),
        scratch_shapes=[pltpu.VMEM((nh, tq, 1), F32), pltpu.VMEM((nh, tq, 1), F32),
                        pltpu.VMEM((nh, tq, kv_lora), F32)],
        compiler_params=_params(("parallel", "parallel", "arbitrary")),
    )(qcat, kcat, kst, wuv)


def _decode_kernel(pt_ref, q_ref, lat_ref, kr_ref, ks_ref, kself_ref, ksself_ref, wuv_ref, o_ref,
                   m_scr, l_scr, acc_scr, *, nh, kv_lora, rope_dim, vdim, scale):
    p = pl.program_id(1)

    @pl.when(p == 0)
    def _():
        m_scr[...] = jnp.full(m_scr.shape, -jnp.inf, F32)
        l_scr[...] = jnp.zeros(l_scr.shape, F32)
        acc_scr[...] = jnp.zeros(acc_scr.shape, F32)

    q = q_ref[0]
    lat = lat_ref[0].astype(BF16)
    s = _dot_nt(q[:, :kv_lora], lat) + _dot_nt(q[:, kv_lora:kv_lora + rope_dim], kr_ref[0].astype(BF16))
    eye = (_iota((nh, nh), 0) == _iota((nh, nh), 1)).astype(F32)
    s = s * (_dot_nt(eye, ks_ref[0], HI) * scale)
    m_old = m_scr[...]
    m_new = jnp.maximum(m_old, jnp.max(s, axis=-1, keepdims=True))
    alpha = jnp.exp(m_old - m_new)
    pr = jnp.exp(s - m_new)
    l_scr[...] = alpha * l_scr[...] + jnp.sum(pr, axis=-1, keepdims=True)
    acc_scr[...] = alpha * acc_scr[...] + _dot(pr.astype(BF16), lat)
    m_scr[...] = m_new

    @pl.when(p == pl.num_programs(1) - 1)
    def _():
        kself = kself_ref[0].astype(F32)
        s_self = jnp.sum(q.astype(F32) * kself, axis=-1, keepdims=True) * ksself_ref[0]
        m_old = m_scr[...]
        m_new = jnp.maximum(m_old, s_self)
        alpha = jnp.exp(m_old - m_new)
        p_self = jnp.exp(s_self - m_new)
        l_fin = alpha * l_scr[...] + p_self
        acc = alpha * acc_scr[...] + p_self.astype(BF16).astype(F32) * kself[:, :kv_lora]
        o_all = _dot((acc / l_fin).astype(BF16), wuv_ref[...])
        for h in range(nh):
            o_ref[0, :, h * vdim:(h + 1) * vdim] = o_all[h:h + 1, h * vdim:(h + 1) * vdim]


def _mla_decode(page_table, qd, cache_lat, cache_kr, cache_ks, kself, ksself, wuv, nh, kv_lora, rope_dim, vdim,
                scale):
    b, n_pages = page_table.shape
    page = cache_lat.shape[1]
    kc = qd.shape[-1]
    grid_spec = pltpu.PrefetchScalarGridSpec(
        num_scalar_prefetch=1,
        grid=(b, n_pages),
        in_specs=[pl.BlockSpec((1, nh, kc), lambda bi, p, pt: (bi, 0, 0)),
                  pl.BlockSpec((1, page, kv_lora), lambda bi, p, pt: (pt[bi, p], 0, 0)),
                  pl.BlockSpec((1, page, rope_dim), lambda bi, p, pt: (pt[bi, p], 0, 0)),
                  pl.BlockSpec((1, page, nh), lambda bi, p, pt: (pt[bi, p], 0, 0)),
                  pl.BlockSpec((1, 1, kc), lambda bi, p, pt: (bi, 0, 0)),
                  pl.BlockSpec((1, nh, 1), lambda bi, p, pt: (bi, 0, 0)),
                  pl.BlockSpec(wuv.shape, lambda bi, p, pt: (0, 0))],
        out_specs=pl.BlockSpec((1, 1, nh * vdim), lambda bi, p, pt: (bi, 0, 0)),
        scratch_shapes=[pltpu.VMEM((nh, 1), F32), pltpu.VMEM((nh, 1), F32), pltpu.VMEM((nh, kv_lora), F32)],
    )
    return pl.pallas_call(
        functools.partial(_decode_kernel, nh=nh, kv_lora=kv_lora, rope_dim=rope_dim, vdim=vdim, scale=scale),
        grid_spec=grid_spec,
        out_shape=jax.ShapeDtypeStruct((b, 1, nh * vdim), F32),
        compiler_params=_params(("parallel", "arbitrary")),
    )(page_table, qd, cache_lat, cache_kr, cache_ks, kself, ksself, wuv)


def _extract_topk(x, k):
    n, tb = x.shape
    rid = _iota((n, tb), 0).astype(F32)
    kid = _iota((k, tb), 0)

    def body(r, carry):
        x, sv = carry
        mx = jnp.max(x, axis=0, keepdims=True)
        first = jnp.min(jnp.where(x == mx, rid, float(n)), axis=0, keepdims=True)
        x = jnp.where(rid == first, -jnp.inf, x)
        sv = jnp.where(kid == r, mx, sv)
        return x, sv

    _, sv = lax.fori_loop(0, k, body, (x, jnp.zeros((k, tb), F32)))
    return sv


def _peer_select_kernel(q_ref, keys_ref, s0_ref, s1_ref, thr_ref, lse_ref, *, nh, half, topk):
    tb = q_ref.shape[0]
    hid = _iota((nh, tb), 0)
    thr_all = jnp.zeros((nh, tb), F32)
    lse_all = jnp.zeros((nh, tb), F32)
    for h in range(nh):
        sv = []
        for p in range(2):
            hp = 2 * h + p
            s = _dot_nt(keys_ref[hp], q_ref[:, hp * half:(hp + 1) * half], HI)
            (s0_ref if p == 0 else s1_ref)[h] = s
            sv.append(_extract_topk(s, topk))
        cand = jnp.concatenate([sv[0][a:a + 1, :] + sv[1] for a in range(topk)], axis=0)
        ts = _extract_topk(cand, topk)
        mx = ts[0:1, :]
        lse = mx + jnp.log(jnp.sum(jnp.exp(ts - mx), axis=0, keepdims=True))
        thr_all = jnp.where(hid == h, ts[topk - 1:topk, :], thr_all)
        lse_all = jnp.where(hid == h, lse, lse_all)
    thr_ref[...] = thr_all
    lse_ref[...] = lse_all


def _peer_select(q, keys, nh, tb):
    t = q.shape[0]
    n_keys, half = keys.shape[1], keys.shape[2]
    return pl.pallas_call(
        functools.partial(_peer_select_kernel, nh=nh, half=half, topk=PEER_TOPK),
        grid=(t // tb,),
        in_specs=[pl.BlockSpec((tb, q.shape[1]), lambda i: (i, 0)),
                  pl.BlockSpec(keys.shape, lambda i: (0, 0, 0))],
        out_specs=[pl.BlockSpec((nh, n_keys, tb), lambda i: (0, 0, i)),
                   pl.BlockSpec((nh, n_keys, tb), lambda i: (0, 0, i)),
                   pl.BlockSpec((nh, tb), lambda i: (0, i)),
                   pl.BlockSpec((nh, tb), lambda i: (0, i))],
        out_shape=[jax.ShapeDtypeStruct((nh, n_keys, t), F32), jax.ShapeDtypeStruct((nh, n_keys, t), F32),
                   jax.ShapeDtypeStruct((nh, t), F32), jax.ShapeDtypeStruct((nh, t), F32)],
        compiler_params=_params(("parallel",)),
    )(q, keys)


def _peer_dense_kernel(xn_ref, u_ref, vt_ref, s0_ref, s1_ref, thr_ref, lse_ref, res_ref, o_ref,
                       ht_scr, a_scr, acc_scr, s0x_scr, *, nh, n_keys, e_blk, t_blk):
    e = pl.program_id(1)
    n_i = e_blk // n_keys
    sqrt_half = float(np.sqrt(0.5))

    @pl.when(e == 0)
    def _():
        acc_scr[...] = jnp.zeros(acc_scr.shape, F32)

    ht_scr[...] = _dot_nt(u_ref[...], xn_ref[...])

    i0 = pl.multiple_of(e * n_i, n_i)
    for h in range(nh):
        rows = s0_ref[h, pl.ds(i0, n_i), :]
        for ii in range(n_i):
            s0x_scr[ii, h:h + 1, :] = rows[ii:ii + 1, :]

    def per_i(ii, carry):
        r0 = pl.multiple_of(ii * n_keys, n_keys)
        for tc in range(t_blk // LANES):
            ls = slice(tc * LANES, (tc + 1) * LANES)
            gate = jnp.zeros((n_keys, LANES), F32)
            for h in range(nh):
                ssum = s1_ref[h, :, ls] + s0x_scr[ii, h:h + 1, ls]
                wgt = jnp.exp(ssum - lse_ref[h:h + 1, ls])
                gate = gate + jnp.where(ssum >= thr_ref[h:h + 1, ls], wgt, 0.0)
            hv = ht_scr[pl.ds(r0, n_keys), ls]
            act = 0.5 * hv * (1.0 + lax.erf(hv * sqrt_half))
            a_scr[pl.ds(r0, n_keys), ls] = (act * gate).astype(BF16)
        return carry

    lax.fori_loop(0, n_i, per_i, 0)
    acc_scr[...] += _dot(vt_ref[...], a_scr[...])

    @pl.when(e == pl.num_programs(1) - 1)
    def _():
        o_ref[...] = res_ref[...] + acc_scr[...].T


def _peer_dense(xn, u, vt, s0t, s1t, thr, lse, resid, t_blk, e_blk):
    t, d = xn.shape
    n_exp = u.shape[0]
    nh, n_keys, _ = s0t.shape
    assert e_blk == 8 * n_keys and nh <= 8
    return pl.pallas_call(
        functools.partial(_peer_dense_kernel, nh=nh, n_keys=n_keys, e_blk=e_blk, t_blk=t_blk),
        grid=(t // t_blk, n_exp // e_blk),
        in_specs=[pl.BlockSpec((t_blk, d), lambda ti, e: (ti, 0)),
                  pl.BlockSpec((e_blk, d), lambda ti, e: (e, 0)),
                  pl.BlockSpec((d, e_blk), lambda ti, e: (0, e)),
                  pl.BlockSpec((nh, n_keys, t_blk), lambda ti, e: (0, 0, ti)),
                  pl.BlockSpec((nh, n_keys, t_blk), lambda ti, e: (0, 0, ti)),
                  pl.BlockSpec((nh, t_blk), lambda ti, e: (0, ti)),
                  pl.BlockSpec((nh, t_blk), lambda ti, e: (0, ti)),
                  pl.BlockSpec((t_blk, d), lambda ti, e: (ti, 0))],
        out_specs=pl.BlockSpec((t_blk, d), lambda ti, e: (ti, 0)),
        out_shape=jax.ShapeDtypeStruct((t, d), F32),
        scratch_shapes=[pltpu.VMEM((e_blk, t_blk), F32), pltpu.VMEM((e_blk, t_blk), BF16),
                        pltpu.VMEM((d, t_blk), F32), pltpu.VMEM((e_blk // n_keys, 8, t_blk), F32)],
        compiler_params=_params(("parallel", "arbitrary")),
    )(xn, u, vt, s0t, s1t, thr, lse, resid)


def _peer(x, gain, wq, keys, u, vt, nh, t_blk):
    q, xn = _mm(x, wq, gain=gain, emit_xn=True)
    s0t, s1t, thr, lse = _peer_select(q, keys, nh, LANES)
    return _peer_dense(xn, u, vt, s0t, s1t, thr, lse, x, t_blk, min(1024, u.shape[0]))


def _rope_tables(pos, rope_dim):
    half = rope_dim // 2
    freq = ROPE_THETA ** (-jnp.arange(half, dtype=F32) / half)
    ang = pos.astype(F32)[:, None] * freq[None, :]
    cos, sin = jnp.cos(ang), jnp.sin(ang)
    z = jnp.zeros_like(cos)
    pad = jnp.zeros((pos.shape[0], LANES - 4 * half), F32)
    cat = lambda *xs: jnp.concatenate(xs + (pad,), axis=1) if pad.shape[1] else jnp.concatenate(xs, axis=1)
    return cat(cos, cos, z, z), cat(-sin, z, z, z), cat(z, sin, z, z)


def _mla_weights(w_in, g_cq, g_ckv, w_uq, g_qn, g_kn, w_uk, w_uv, w_out):
    d = w_in.shape[0]
    q_lora, kv_lora = g_cq.shape[0], g_ckv.shape[0]
    rope_dim = w_in.shape[1] - q_lora - kv_lora
    _, nh, nope = w_uk.shape
    vdim = w_uv.shape[2]
    qk_dim = nope + rope_dim
    assert nope == LANES and 2 * rope_dim <= LANES and rope_dim % 2 == 0 and nh <= 8
    w_in_p = jnp.concatenate([w_in, jnp.zeros((d, LANES - rope_dim), F32)], axis=1).astype(BF16)
    wq3 = w_uq.reshape(q_lora, nh, qk_dim)
    w_nope = wq3[:, :, :nope].reshape(q_lora, nh * nope)
    w_rope = jnp.pad(wq3[:, :, nope:], ((0, 0), (0, 0), (0, nope - rope_dim))).reshape(q_lora, nh * nope)
    gg = g_qn * g_kn
    hn = nh * nope
    head_of = np.arange(hn) // nope
    seg = np.zeros((hn, LANES), np.float32)
    seg[np.arange(hn), head_of] = 1.0
    seg8 = np.zeros((8, hn), np.float32)
    seg8[head_of, np.arange(hn)] = 1.0
    return dict(
        nh=nh, nope=nope, rope=rope_dim, q_lora=q_lora, kv_lora=kv_lora, vdim=vdim,
        w_in=w_in_p,
        gcq=g_cq.reshape(1, -1), gckv=g_ckv.reshape(1, -1),
        wuq=jnp.concatenate([w_nope, w_rope], axis=1).astype(BF16),
        gqn=jnp.tile(gg[:nope], nh).reshape(1, hn),
        gqr=jnp.tile(jnp.pad(gg[nope:], (0, nope - rope_dim)), nh).reshape(1, hn),
        wuk=w_uk.reshape(kv_lora, hn).astype(BF16),
        wuv=w_uv.reshape(kv_lora, nh * vdim).astype(BF16),
        w_out=w_out.astype(BF16),
        seg=jnp.asarray(seg), segt=jnp.asarray(seg.T.copy()), seg8=jnp.asarray(seg8),
    )


def kernel(x_prompt, x_sample, cache_latent, cache_krope, cache_kscale, page_table, state_ssm, state_conv,
           norm_mix, norm_ffn, gdn_w_in, gdn_conv_w, gdn_a_log, gdn_dt_bias, gdn_out_norm, gdn_w_out,
           mla_w_in, mla_cq_norm, mla_ckv_norm, mla_w_uq, mla_q_norm, mla_k_norm, mla_w_uk, mla_w_uv, mla_w_out,
           peer_w_q, peer_sub_keys, peer_u, peer_v):
    bp, seq, d = x_prompt.shape
    bs, dec_seq, _ = x_sample.shape
    assert dec_seq == 1 and bs <= SAMPLE_PAD
    depth = norm_mix.shape[0]
    n_pages, page = page_table.shape[1], cache_latent.shape[2]
    past_len = n_pages * page
    gdn_heads, gdn_dk = state_ssm.shape[2], state_ssm.shape[3]
    gdn_dim = gdn_heads * gdn_dk
    peer_heads, n_keys, peer_half = peer_sub_keys.shape[1], peer_sub_keys.shape[3], peer_sub_keys.shape[4]
    assert n_keys == LANES and peer_half == LANES and gdn_dk == LANES

    tp = bp * seq
    xp = x_prompt.reshape(tp, d)
    xs = jnp.pad(x_sample.reshape(bs, d), ((0, SAMPLE_PAD - bs), (0, 0)))
    gdn_tb = min(256, seq)
    attn_tq = min(512, seq)
    peer_tblk_p = min(512, tp)

    rope_dim = mla_w_in.shape[2] - mla_cq_norm.shape[1] - mla_ckv_norm.shape[1]
    tabs_p = _rope_tables(jnp.tile(jnp.arange(seq), bp), rope_dim)
    tabs_s = _rope_tables(jnp.full((SAMPLE_PAD,), past_len), rope_dim)

    lat_p, kr_p, ksc_p, ssm_p, conv_p = [], [], [], [], []
    lat_s, kr_s, ksc_s, ssm_s, conv_s = [], [], [], [], []
    for i in range(depth):
        j = i // 2
        if i % 2 == 0:
            w_in = gdn_w_in[j]
            n_extra = w_in.shape[1] - 4 * gdn_dim
            w_in_p = jnp.concatenate([w_in, jnp.zeros((d, LANES - n_extra), F32)], axis=1).astype(BF16)
            w_out = gdn_w_out[j].astype(BF16)
            proj = _mm(xp, w_in_p, gain=norm_mix[i], tn_cap=1408).reshape(bp, seq, -1)
            rows = _gdn_rows(proj[..., 4 * gdn_dim:], gdn_heads, GDN_CHUNK)
            o, sfin = _gdn_mixer(proj, rows, gdn_a_log[j], gdn_dt_bias[j], gdn_conv_w[j], gdn_out_norm[j],
                                 jnp.zeros((bp, CONV_W - 1, 3 * gdn_dim), F32),
                                 jnp.zeros((bp, gdn_heads, gdn_dk, gdn_dk), F32), None, gdn_tb)
            xp = _mm(o.reshape(tp, gdn_dim), w_out, resid=xp)
            ssm_p.append(sfin)
            conv_p.append(proj[:, seq - (CONV_W - 1):, :3 * gdn_dim])
            proj_s = _mm(xs, w_in_p, gain=norm_mix[i], tn_cap=1408)[:bs]
            proj_c = jnp.pad(proj_s[:, None, :], ((0, 0), (0, GDN_CHUNK - 1), (0, 0)))
            rows_s = _gdn_rows(proj_c[..., 4 * gdn_dim:], gdn_heads, GDN_CHUNK)
            o_s, sfin_s = _gdn_mixer(proj_c, rows_s, gdn_a_log[j], gdn_dt_bias[j], gdn_conv_w[j], gdn_out_norm[j],
                                     state_conv[j], state_ssm[j], 1, GDN_CHUNK)
            o_s = jnp.pad(o_s[:, 0, :], ((0, SAMPLE_PAD - bs), (0, 0)))
            xs = _mm(o_s, w_out, resid=xs)
            ssm_s.append(sfin_s)
            conv_s.append(jnp.concatenate([state_conv[j][:, 1:, :], proj_s[:, None, :3 * gdn_dim]], axis=1))
        else:
            w = _mla_weights(mla_w_in[j], mla_cq_norm[j], mla_ckv_norm[j], mla_w_uq[j], mla_q_norm[j],
                             mla_k_norm[j], mla_w_uk[j], mla_w_uv[j], mla_w_out[j])
            nh, kv_lora, vdim = w["nh"], w["kv_lora"], w["vdim"]
            kc = kv_lora + LANES
            proj = _mm(xp, w["w_in"], gain=norm_mix[i])
            lat, kr, ksc, qcat, kcat, kst = _mla_prep(proj, w, *tabs_p, tm=min(256, tp))
            kst_b = kst.reshape(8, bp, seq).transpose(1, 0, 2)
            o = _mla_attention(qcat.reshape(nh, bp, seq, kc), kcat.reshape(bp, seq, kc), kst_b, w["wuv"],
                               nh, kv_lora, vdim, attn_tq)
            xp = _mm(o.reshape(tp, nh * vdim), w["w_out"], resid=xp)
            lat_p.append(lat.reshape(bp, seq, kv_lora))
            kr_p.append(kr.reshape(bp, seq, -1))
            ksc_p.append(ksc.reshape(bp, seq, nh))
            proj_s = _mm(xs, w["w_in"], gain=norm_mix[i])
            lat1, kr1, ksc1, qcat1, kcat1, kst1 = _mla_prep(proj_s, w, *tabs_s, tm=SAMPLE_PAD)
            qd = qcat1[:, :bs, :].transpose(1, 0, 2)
            ksself = kst1[:nh, :bs].T.reshape(bs, nh, 1)
            o_s = _mla_decode(page_table, qd, cache_latent[j], cache_krope[j], cache_kscale[j],
                              kcat1[:bs].reshape(bs, 1, kc), ksself, w["wuv"], nh, kv_lora, w["rope"], vdim,
                              float((w["nope"] + w["rope"]) ** -0.5))
            o_s = jnp.pad(o_s.reshape(bs, nh * vdim), ((0, SAMPLE_PAD - bs), (0, 0)))
            xs = _mm(o_s, w["w_out"], resid=xs)
            lat_s.append(lat1[:bs].reshape(bs, 1, kv_lora))
            kr_s.append(kr1[:bs].reshape(bs, 1, -1))
            ksc_s.append(ksc1[:bs].reshape(bs, 1, nh))
        wq = peer_w_q[i].astype(BF16)
        keys = peer_sub_keys[i].reshape(peer_heads * 2, n_keys, peer_half)
        u = peer_u[i].astype(BF16)
        vt = peer_v[i].astype(BF16).T
        xp = _peer(xp, norm_ffn[i], wq, keys, u, vt, peer_heads, peer_tblk_p)
        xs = _peer(xs, norm_ffn[i], wq, keys, u, vt, peer_heads, SAMPLE_PAD)
    return (xp.reshape(bp, seq, d), xs[:bs].reshape(bs, 1, d),
            jnp.stack(lat_p), jnp.stack(kr_p), jnp.stack(ksc_p), jnp.stack(ssm_p), jnp.stack(conv_p),
            jnp.stack(lat_s), jnp.stack(kr_s), jnp.stack(ksc_s), jnp.stack(ssm_s), jnp.stack(conv_s))
```

```python
import functools
import math

import numpy as np
import jax
import jax.numpy as jnp
from jax import lax
from jax.experimental import pallas as pl
from jax.experimental.pallas import tpu as pltpu

F32 = jnp.float32
BF16 = jnp.bfloat16
HI = lax.Precision.HIGHEST

EPS = 1e-6
ROPE_THETA = 10000.0
GDN_CHUNK = 64
CONV_W = 4
PEER_TOPK = 16
LANES = 128
SUBLANES = 8
VMEM_LIMIT = 52 * 1024 * 1024
SAMPLE_PAD = 128
PEER_SLICE = 256

NT_DIMS = (((1,), (1,)), ((), ()))
TN_DIMS = (((0,), (0,)), ((), ()))


def _params(sem):
    return pltpu.CompilerParams(dimension_semantics=sem, vmem_limit_bytes=VMEM_LIMIT)


def _iota(shape, dim):
    return lax.broadcasted_iota(jnp.int32, shape, dim)


def _aligned(x, m):
    return x if isinstance(x, int) else pl.multiple_of(x, m)


def _silu(x):
    return x * jax.nn.sigmoid(x)


def _softplus(x):
    return jnp.maximum(x, 0.0) + jnp.log1p(jnp.exp(-jnp.abs(x)))


def _dot(a, b, precision=None):
    return jnp.dot(a, b, preferred_element_type=F32, precision=precision)


def _dot_nt(a, b, precision=None):
    return lax.dot_general(a, b, NT_DIMS, preferred_element_type=F32, precision=precision)


def _mm_kernel(*refs, has_gain, has_resid, emit_xn):
    it = iter(refs)
    x_ref, w_ref = next(it), next(it)
    g_ref = next(it) if has_gain else None
    r_ref = next(it) if has_resid else None
    o_ref = next(it)
    xn_ref = next(it) if emit_xn else None
    xb_scr = next(it)

    @pl.when(pl.program_id(1) == 0)
    def _():
        x = x_ref[...].astype(F32)
        if has_gain:
            x = x * lax.rsqrt(jnp.mean(x * x, axis=-1, keepdims=True) + EPS) * g_ref[...]
        xb = x.astype(BF16)
        xb_scr[...] = xb
        if emit_xn:
            xn_ref[...] = xb

    y = _dot(xb_scr[...], w_ref[...])
    if has_resid:
        y = y + r_ref[...]
    o_ref[...] = y.astype(o_ref.dtype)


def _pick_tile(n, cap):
    if n <= cap:
        return n
    best = LANES
    for t in range(LANES, cap + 1, LANES):
        if n % t == 0:
            best = t
    return best


def _mm(x, w, gain=None, resid=None, emit_xn=False, tm_cap=1024, tn_cap=1024):
    m, k = x.shape
    n = w.shape[1]
    tm = _pick_tile(m, tm_cap)
    tn = _pick_tile(n, tn_cap)
    assert m % tm == 0 and n % tn == 0
    in_specs = [pl.BlockSpec((tm, k), lambda i, j: (i, 0)),
                pl.BlockSpec((k, tn), lambda i, j: (0, j))]
    args = [x, w]
    if gain is not None:
        in_specs.append(pl.BlockSpec((1, k), lambda i, j: (0, 0)))
        args.append(gain.reshape(1, k).astype(F32))
    if resid is not None:
        in_specs.append(pl.BlockSpec((tm, tn), lambda i, j: (i, j)))
        args.append(resid)
    out_shape = [jax.ShapeDtypeStruct((m, n), F32)]
    out_specs = [pl.BlockSpec((tm, tn), lambda i, j: (i, j))]
    if emit_xn:
        out_shape.append(jax.ShapeDtypeStruct((m, k), BF16))
        out_specs.append(pl.BlockSpec((tm, k), lambda i, j: (i, 0)))
    res = pl.pallas_call(
        functools.partial(_mm_kernel, has_gain=gain is not None, has_resid=resid is not None, emit_xn=emit_xn),
        grid=(m // tm, n // tn),
        in_specs=in_specs,
        out_specs=out_specs,
        out_shape=out_shape,
        scratch_shapes=[pltpu.VMEM((tm, k), BF16)],
        compiler_params=_params(("parallel", "arbitrary")),
        name="norm_matmul",
    )(*args)
    return res if emit_xn else res[0]


def _unit_lower_inverse(low, eye, ri, ci, chunk):
    base = min(16, chunk)
    assert chunk & (chunk - 1) == 0
    same = lambda size: (ri >> (size.bit_length() - 1)) == (ci >> (size.bit_length() - 1))
    pw = [-jnp.where(same(base), l, 0.0) for l in low]
    inv = [eye + p for p in pw]
    for _ in range(max(int(math.ceil(math.log2(base))) - 1, 0)):
        pw = [_dot(p, p, HI) for p in pw]
        inv = [t + _dot(t, p, HI) for t, p in zip(inv, pw)]
    size = base
    while size < chunk:
        off = jnp.logical_and(same(2 * size), jnp.logical_not(same(size)))
        x = [_dot(t, jnp.where(off, l, 0.0), HI) for t, l in zip(inv, low)]
        x = [_dot(xi, t, HI) for xi, t in zip(x, inv)]
        inv = [t - xi for t, xi in zip(inv, x)]
        size *= 2
    return inv


def _gdn_kernel(q_ref, k_ref, v_ref, z_ref, rows_ref, alog_ref, dt_ref, cwq_ref, cwk_ref, cwv_ref,
                gout_ref, bq_ref, bk_ref, bv_ref, s0_ref, o_ref, sfin_ref,
                xq_scr, xk_scr, xv_scr, s_scr, *, tb, chunk, hb, valid_len):
    t = pl.program_id(2)
    dk = s_scr.shape[-1]
    hist = CONV_W - 1
    base = SUBLANES - hist
    ncb = tb // chunk

    @pl.when(t == 0)
    def _():
        s_scr[...] = s0_ref[0]
        xq_scr[base:SUBLANES, :] = bq_ref[0]
        xk_scr[base:SUBLANES, :] = bk_ref[0]
        xv_scr[base:SUBLANES, :] = bv_ref[0]

    def conv_silu(x_ref, w_ref, scr):
        scr[SUBLANES:SUBLANES + tb, :] = x_ref[0]
        y = w_ref[0:1, :] * scr[base:base + tb, :]
        for j in range(1, CONV_W):
            y = y + w_ref[j:j + 1, :] * scr[base + j:base + j + tb, :]
        scr[base:SUBLANES, :] = scr[tb + base:tb + SUBLANES, :]
        return _silu(y)

    def l2n(x):
        return x * lax.rsqrt(jnp.sum(x * x, axis=-1, keepdims=True) + EPS)

    q_all = conv_silu(q_ref, cwq_ref, xq_scr)
    k_all = conv_silu(k_ref, cwk_ref, xk_scr)
    v_all = conv_silu(v_ref, cwv_ref, xv_scr)

    ri = _iota((chunk, chunk), 0)
    ci = _iota((chunk, chunk), 1)
    eye = (ri == ci).astype(F32)
    incl = ri >= ci
    strict = ri > ci
    tril_f = incl.astype(F32)
    triu_f = (ri <= ci).astype(F32)
    rid8 = _iota((SUBLANES, chunk), 0)

    units = [(hh, c) for c in range(ncb) for hh in range(hb)]
    qs, ks, vs, beta, g_col, decay, low = [], [], [], [], [], [], []
    for hh, c in units:
        hs = slice(hh * dk, (hh + 1) * dk)
        r0 = c * chunk
        qs.append(l2n(q_all[r0:r0 + chunk, hs]) * (dk ** -0.5))
        ks.append(l2n(k_all[r0:r0 + chunk, hs]))
        vs.append(v_all[r0:r0 + chunk, hs])
        raw = rows_ref[0, hh, c]
        m8 = jnp.where(rid8 == 0, jax.nn.sigmoid(raw),
                       jnp.where(rid8 == 1, -jnp.exp(alog_ref[hh]) * _softplus(raw + dt_ref[hh]), 0.0))
        if valid_len is not None:
            pos = t * tb + r0 + _iota((SUBLANES, chunk), 1)
            m8 = jnp.where(pos < valid_len, m8, 0.0)
        g_row = _dot(m8, triu_f, HI)[1:2, :]
        beta.append(_dot_nt(eye, m8, HI)[:, 0:1])
        g_col.append(_dot_nt(tril_f, m8, HI)[:, 1:2])
        decay.append(jnp.exp(jnp.where(incl, g_col[-1] - g_row, -jnp.inf)))
    kb = [x.astype(BF16) for x in ks]
    for n in range(len(units)):
        low.append(jnp.where(strict, beta[n] * _dot_nt(kb[n], kb[n]) * decay[n], 0.0))
    inv = _unit_lower_inverse(low, eye, ri, ci, chunk)
    e_g = [jnp.exp(g) for g in g_col]
    w = [_dot(inv[n], jnp.concatenate([vs[n] * beta[n], ks[n] * (beta[n] * e_g[n])], axis=1), HI)
         for n in range(len(units))]
    qk = [_dot_nt(qs[n].astype(BF16), kb[n]) * decay[n] for n in range(len(units))]

    for n, (hh, c) in enumerate(units):
        hs = slice(hh * dk, (hh + 1) * dk)
        r0 = c * chunk
        g_last = g_col[n][chunk - 1:chunk, :]
        q_dec = qs[n] * e_g[n]
        k_dec = ks[n] * jnp.exp(g_last - g_col[n])
        s = s_scr[hh]
        sb = s.astype(BF16)
        u = w[n][:, :dk] - _dot(w[n][:, dk:].astype(BF16), sb)
        ub = u.astype(BF16)
        o = _dot(q_dec.astype(BF16), sb) + _dot(qk[n].astype(BF16), ub)
        s_scr[hh] = s * jnp.exp(g_last) + lax.dot_general(k_dec.astype(BF16), ub, TN_DIMS,
                                                           preferred_element_type=F32)
        on = o * lax.rsqrt(jnp.mean(o * o, axis=-1, keepdims=True) + EPS) * gout_ref[...]
        o_ref[0, r0:r0 + chunk, hs] = on * _silu(z_ref[0, r0:r0 + chunk, hs])

    @pl.when(t == pl.num_programs(2) - 1)
    def _():
        sfin_ref[0] = s_scr[...]


def _gdn_mixer(proj, rows, a_log, dt_bias, conv_w, g_out, buf0, s0, valid_len, tb, hb):
    b, t, _ = proj.shape
    nh = s0.shape[1]
    dk = s0.shape[2]
    chunk = rows.shape[-1]
    dim = nh * dk
    ng = nh // hb
    wdt = hb * dk
    col = lambda off: pl.BlockSpec((1, tb, wdt), lambda bi, h, ti, off=off: (bi, ti, off + h))
    cw = lambda off: pl.BlockSpec((CONV_W, wdt), lambda bi, h, ti, off=off: (0, off + h))
    bf = lambda off: pl.BlockSpec((1, CONV_W - 1, wdt), lambda bi, h, ti, off=off: (bi, 0, off + h))
    hscal = pl.BlockSpec((hb, 1, 1), lambda bi, h, ti: (h, 0, 0))
    o, sfin = pl.pallas_call(
        functools.partial(_gdn_kernel, tb=tb, chunk=chunk, hb=hb, valid_len=valid_len),
        grid=(b, ng, t // tb),
        in_specs=[col(0), col(ng), col(2 * ng), col(3 * ng),
                  pl.BlockSpec((1, hb, tb // chunk, SUBLANES, chunk), lambda bi, h, ti: (bi, h, ti, 0, 0)),
                  hscal, hscal, cw(0), cw(ng), cw(2 * ng),
                  pl.BlockSpec((1, dk), lambda bi, h, ti: (0, 0)),
                  bf(0), bf(ng), bf(2 * ng),
                  pl.BlockSpec((1, hb, dk, dk), lambda bi, h, ti: (bi, h, 0, 0))],
        out_specs=[pl.BlockSpec((1, tb, wdt), lambda bi, h, ti: (bi, ti, h)),
                   pl.BlockSpec((1, hb, dk, dk), lambda bi, h, ti: (bi, h, 0, 0))],
        out_shape=[jax.ShapeDtypeStruct((b, t, dim), F32),
                   jax.ShapeDtypeStruct((b, nh, dk, dk), F32)],
        scratch_shapes=[pltpu.VMEM((tb + SUBLANES, wdt), F32)] * 3 + [pltpu.VMEM((hb, dk, dk), F32)],
        compiler_params=_params(("parallel", "parallel", "arbitrary")),
        name="gdn_chunked",
    )(proj, proj, proj, proj, rows, a_log.reshape(nh, 1, 1), dt_bias.reshape(nh, 1, 1),
      conv_w, conv_w, conv_w, g_out.reshape(1, dk), buf0, buf0, buf0, s0)
    return o, sfin


def _gdn_rows(ba, nh, chunk):
    b, t, _ = ba.shape
    x = jnp.stack([ba[..., :nh], ba[..., nh:2 * nh]], axis=-1)
    x = x.reshape(b, t // chunk, chunk, nh, 2).transpose(0, 3, 1, 4, 2)
    return jnp.pad(x, ((0, 0),) * 3 + ((0, SUBLANES - 2), (0, 0)))


def _rope_block(x, cos_t, sin_a, sin_b):
    q4 = x.shape[-1] // 4
    return x * cos_t + pltpu.roll(x, 3 * q4, 1) * sin_a + pltpu.roll(x, q4, 1) * sin_b


def _mla_prep_kernel(p_ref, gcq_ref, gckv_ref, wuq_ref, gqn_ref, gqr_ref, wuk_ref, seg_ref, segt_ref,
                     seg8_ref, cos_ref, sina_ref, sinb_ref,
                     lat_ref, kr_ref, ksc_ref, qcat_ref, kcat_ref, kst_ref,
                     *, q_lora, kv_lora, nh, nope, rope_dim, scale):
    qk_dim = nope + rope_dim
    hn = nh * nope
    x = p_ref[...]
    cq = x[:, :q_lora]
    cq = cq * lax.rsqrt(jnp.mean(cq * cq, axis=-1, keepdims=True) + EPS) * gcq_ref[...]
    ckv = x[:, q_lora:q_lora + kv_lora]
    ckv = ckv * lax.rsqrt(jnp.mean(ckv * ckv, axis=-1, keepdims=True) + EPS) * gckv_ref[...]
    cos_t, sin_a, sin_b = cos_ref[...], sina_ref[...], sinb_ref[...]
    krb = _rope_block(x[:, q_lora + kv_lora:], cos_t, sin_a, sin_b)
    lat_ref[...] = ckv
    kr_ref[...] = krb[:, :rope_dim]
    ckv_b = ckv.astype(BF16)
    kcat_ref[:, :kv_lora] = ckv_b
    kcat_ref[:, kv_lora:] = krb.astype(BF16)

    qf = _dot(cq.astype(BF16), wuq_ref[...])
    qn = qf[:, :hn]
    qr = jnp.concatenate([_rope_block(qf[:, hn + h * nope:hn + (h + 1) * nope], cos_t, sin_a, sin_b)
                          for h in range(nh)], axis=1)
    ssq = _dot(qn * qn + qr * qr, seg_ref[...], HI)
    inv = lax.rsqrt(ssq * (1.0 / qk_dim) + EPS)
    inv_x = _dot(inv, segt_ref[...], HI)
    qn = qn * inv_x * gqn_ref[...]
    qr = qr * inv_x * gqr_ref[...]

    k_nope = _dot(ckv_b, wuk_ref[...])
    k2 = k_nope * k_nope
    kr2 = krb * krb
    ksq = _dot(k2, seg_ref[...], HI) + jnp.sum(kr2, axis=-1, keepdims=True)
    ksc = lax.rsqrt(ksq * (1.0 / qk_dim) + EPS)
    ksc_ref[...] = ksc[:, :nh]
    ksq_t = _dot_nt(seg8_ref[...], k2, HI) + _dot_nt(jnp.ones((SUBLANES, kr2.shape[1]), F32), kr2, HI)
    kst_ref[...] = lax.rsqrt(ksq_t * (1.0 / qk_dim) + EPS) * scale

    for h in range(nh):
        sl = slice(h * nope, (h + 1) * nope)
        q_lat = _dot_nt(qn[:, sl].astype(BF16), wuk_ref[:, sl])
        qcat_ref[h, :, :kv_lora] = q_lat.astype(BF16)
        qcat_ref[h, :, kv_lora:] = qr[:, sl].astype(BF16)


def _mla_prep(proj, w, cos_t, sin_a, sin_b, tm):
    m = proj.shape[0]
    nh, nope, rope_dim, q_lora, kv_lora = w["nh"], w["nope"], w["rope"], w["q_lora"], w["kv_lora"]
    kc = kv_lora + LANES
    full = lambda a: pl.BlockSpec(a.shape, lambda i: (0,) * a.ndim)
    rows = lambda n: pl.BlockSpec((tm, n), lambda i: (i, 0))
    consts = [w["gcq"], w["gckv"], w["wuq"], w["gqn"], w["gqr"], w["wuk"], w["seg"], w["segt"], w["seg8"]]
    return pl.pallas_call(
        functools.partial(_mla_prep_kernel, q_lora=q_lora, kv_lora=kv_lora, nh=nh, nope=nope,
                          rope_dim=rope_dim, scale=float((nope + rope_dim) ** -0.5)),
        grid=(m // tm,),
        in_specs=[rows(proj.shape[1])] + [full(a) for a in consts] + [rows(LANES)] * 3,
        out_specs=[rows(kv_lora), rows(rope_dim), rows(nh),
                   pl.BlockSpec((nh, tm, kc), lambda i: (0, i, 0)), rows(kc),
                   pl.BlockSpec((SUBLANES, tm), lambda i: (0, i))],
        out_shape=[jax.ShapeDtypeStruct((m, kv_lora), F32), jax.ShapeDtypeStruct((m, rope_dim), F32),
                   jax.ShapeDtypeStruct((m, nh), F32), jax.ShapeDtypeStruct((nh, m, kc), BF16),
                   jax.ShapeDtypeStruct((m, kc), BF16), jax.ShapeDtypeStruct((SUBLANES, m), F32)],
        compiler_params=_params(("parallel",)),
        name="mla_prep",
    )(proj, *consts, cos_t, sin_a, sin_b)


def _attn_kernel(q_ref, k_ref, ks_ref, wuv_ref, o_ref, m_scr, l_scr, acc_scr, *, nh, kv_lora, vdim, tq, tk):
    qi, kj = pl.program_id(1), pl.program_id(2)

    @pl.when(kj == 0)
    def _():
        m_scr[...] = jnp.full(m_scr.shape, -jnp.inf, F32)
        l_scr[...] = jnp.zeros(l_scr.shape, F32)
        acc_scr[...] = jnp.zeros(acc_scr.shape, F32)

    @pl.when(kj <= qi)
    def _():
        kblk = k_ref[0]
        vblk = kblk[:, :kv_lora]
        causal = (kj * tk + _iota((tq, tk), 1)) <= (qi * tq + _iota((tq, tk), 0))
        for h in range(nh):
            s = _dot_nt(q_ref[h, 0], kblk) * ks_ref[0, h:h + 1, :]
            s = jnp.where(causal, s, -jnp.inf)
            m_old = m_scr[h]
            m_new = jnp.maximum(m_old, jnp.max(s, axis=-1, keepdims=True))
            alpha = jnp.exp(m_old - m_new)
            p = jnp.exp(s - m_new)
            l_scr[h] = alpha * l_scr[h] + jnp.sum(p, axis=-1, keepdims=True)
            acc_scr[h] = alpha * acc_scr[h] + _dot(p.astype(BF16), vblk)
            m_scr[h] = m_new

    @pl.when(kj == qi)
    def _():
        for h in range(nh):
            o_lat = acc_scr[h] / l_scr[h]
            o_ref[0, :, h * vdim:(h + 1) * vdim] = _dot(o_lat.astype(BF16), wuv_ref[:, h * vdim:(h + 1) * vdim])


def _mla_attention(qcat, kcat, kst, wuv, nh, kv_lora, vdim, tq):
    _, b, t, kc = qcat.shape
    tk = tq
    return pl.pallas_call(
        functools.partial(_attn_kernel, nh=nh, kv_lora=kv_lora, vdim=vdim, tq=tq, tk=tk),
        grid=(b, t // tq, t // tk),
        in_specs=[pl.BlockSpec((nh, 1, tq, kc), lambda bi, qi, kj: (0, bi, qi, 0)),
                  pl.BlockSpec((1, tk, kc), lambda bi, qi, kj: (bi, jnp.minimum(kj, qi), 0)),
                  pl.BlockSpec((1, SUBLANES, tk), lambda bi, qi, kj: (bi, 0, jnp.minimum(kj, qi))),
                  pl.BlockSpec(wuv.shape, lambda bi, qi, kj: (0, 0))],
        out_specs=pl.BlockSpec((1, tq, nh * vdim), lambda bi, qi, kj: (bi, qi, 0)),
        out_shape=jax.ShapeDtypeStruct((b, t, nh * vdim), F32),
        scratch_shapes=[pltpu.VMEM((nh, tq, 1), F32), pltpu.VMEM((nh, tq, 1), F32),
                        pltpu.VMEM((nh, tq, kv_lora), F32)],
        compiler_params=_params(("parallel", "parallel", "arbitrary")),
        name="mla_prefill_attention",
    )(qcat, kcat, kst, wuv)


def _decode_kernel(pt_ref, q_ref, *refs, nh, kv_lora, rope_dim, vdim, scale, npg):
    lat_refs, kr_refs, ks_refs = refs[:npg], refs[npg:2 * npg], refs[2 * npg:3 * npg]
    kself_ref, ksself_ref, wuv_ref, o_ref, m_scr, l_scr, acc_scr = refs[3 * npg:]
    p = pl.program_id(1)

    @pl.when(p == 0)
    def _():
        m_scr[...] = jnp.full(m_scr.shape, -jnp.inf, F32)
        l_scr[...] = jnp.zeros(l_scr.shape, F32)
        acc_scr[...] = jnp.zeros(acc_scr.shape, F32)

    q = q_ref[0]
    q_lat, q_rope = q[:, :kv_lora], q[:, kv_lora:kv_lora + rope_dim]
    lat = [r[0].astype(BF16) for r in lat_refs]
    s = jnp.concatenate(
        [(_dot_nt(q_lat, lat[g]) + _dot_nt(q_rope, kr_refs[g][0].astype(BF16))) * (ks_refs[g][0] * scale)
         for g in range(npg)], axis=1)
    m_old = m_scr[...]
    m_new = jnp.maximum(m_old, jnp.max(s, axis=-1, keepdims=True))
    alpha = jnp.exp(m_old - m_new)
    pr = jnp.exp(s - m_new)
    l_scr[...] = alpha * l_scr[...] + jnp.sum(pr, axis=-1, keepdims=True)
    page = lat[0].shape[0]
    pv = _dot(pr[:, :page].astype(BF16), lat[0])
    for g in range(1, npg):
        pv = pv + _dot(pr[:, g * page:(g + 1) * page].astype(BF16), lat[g])
    acc_scr[...] = alpha * acc_scr[...] + pv
    m_scr[...] = m_new

    @pl.when(p == pl.num_programs(1) - 1)
    def _():
        kself = kself_ref[0].astype(F32)
        s_self = jnp.sum(q.astype(F32) * kself, axis=-1, keepdims=True) * ksself_ref[0]
        m_old = m_scr[...]
        m_new = jnp.maximum(m_old, s_self)
        alpha = jnp.exp(m_old - m_new)
        p_self = jnp.exp(s_self - m_new)
        l_fin = alpha * l_scr[...] + p_self
        acc = alpha * acc_scr[...] + p_self.astype(BF16).astype(F32) * kself[:, :kv_lora]
        o_all = _dot((acc / l_fin).astype(BF16), wuv_ref[...])
        for h in range(nh):
            o_ref[0, :, h * vdim:(h + 1) * vdim] = o_all[h:h + 1, h * vdim:(h + 1) * vdim]


def _mla_decode(page_table, qd, cache_lat, cache_kr, cache_kst, kself, ksself, wuv, nh, kv_lora, rope_dim, vdim,
                scale, npg):
    b, n_pages = page_table.shape
    page = cache_lat.shape[1]
    kc = qd.shape[-1]
    assert n_pages % npg == 0
    pg = lambda shape, g: pl.BlockSpec(shape, lambda bi, p, pt, g=g: (pt[bi, p * npg + g], 0, 0))
    grid_spec = pltpu.PrefetchScalarGridSpec(
        num_scalar_prefetch=1,
        grid=(b, n_pages // npg),
        in_specs=([pl.BlockSpec((1, nh, kc), lambda bi, p, pt: (bi, 0, 0))]
                  + [pg((1, page, kv_lora), g) for g in range(npg)]
                  + [pg((1, page, rope_dim), g) for g in range(npg)]
                  + [pg((1, nh, page), g) for g in range(npg)]
                  + [pl.BlockSpec((1, 1, kc), lambda bi, p, pt: (bi, 0, 0)),
                     pl.BlockSpec((1, nh, 1), lambda bi, p, pt: (bi, 0, 0)),
                     pl.BlockSpec(wuv.shape, lambda bi, p, pt: (0, 0))]),
        out_specs=pl.BlockSpec((1, 1, nh * vdim), lambda bi, p, pt: (bi, 0, 0)),
        scratch_shapes=[pltpu.VMEM((nh, 1), F32), pltpu.VMEM((nh, 1), F32), pltpu.VMEM((nh, kv_lora), F32)],
    )
    return pl.pallas_call(
        functools.partial(_decode_kernel, nh=nh, kv_lora=kv_lora, rope_dim=rope_dim, vdim=vdim, scale=scale,
                          npg=npg),
        grid_spec=grid_spec,
        out_shape=jax.ShapeDtypeStruct((b, 1, nh * vdim), F32),
        compiler_params=_params(("parallel", "arbitrary")),
        name="mla_paged_decode",
    )(page_table, qd, *([cache_lat] * npg), *([cache_kr] * npg), *([cache_kst] * npg), kself, ksself, wuv)


def _oddeven_merge_sort_pairs(n):
    pairs = []

    def merge(lo, m, r):
        step = r * 2
        if step < m:
            merge(lo, m, step)
            merge(lo + r, m, step)
            pairs.extend((i, i + r) for i in range(lo + r, lo + m - r, step))
        else:
            pairs.append((lo, lo + r))

    def sort(lo, m):
        if m > 1:
            sort(lo, m // 2)
            sort(lo + m // 2, m // 2)
            merge(lo, m, 1)

    sort(0, n)
    return pairs


def _cmp_exchange(v, i, j):
    a, b = v[i], v[j]
    if b is None:
        return
    if a is None:
        v[i], v[j] = b, None
        return
    v[i], v[j] = jnp.maximum(a, b), jnp.minimum(a, b)


def _top_sorted(v):
    k = len(v)
    v = list(v)
    for i, j in _oddeven_merge_sort_pairs(k):
        _cmp_exchange(v, i, j)
    shift = SUBLANES // 2
    while shift >= 1:
        rot = [None if x is None else pltpu.roll(x, shift, 0) for x in v]
        v = [rot[k - 1 - i] if v[i] is None else (v[i] if rot[k - 1 - i] is None
                                                   else jnp.maximum(v[i], rot[k - 1 - i])) for i in range(k)]
        d = k // 2
        while d >= 1:
            for i in range(k):
                if i & d == 0:
                    _cmp_exchange(v, i, i + d)
            d //= 2
        shift //= 2
    return v


def _peer_select_kernel(q_ref, keys_ref, s0_ref, s1_ref, e0_ref, e1_ref, thr_ref, *, nh, half, topk):
    tb = q_ref.shape[0]
    n_keys = keys_ref.shape[1]
    sub = _iota((SUBLANES, tb), 0)
    hid = _iota((nh, tb), 0)
    neg = jnp.full((SUBLANES, tb), -jnp.inf, F32)

    def spread(vals):
        out = vals[SUBLANES - 1]
        for r in range(SUBLANES - 2, -1, -1):
            out = jnp.where(sub == r, vals[r], out)
        return out

    thr_all = jnp.zeros((nh, tb), F32)
    for h in range(nh):
        s, sv = [], []
        for p in range(2):
            hp = 2 * h + p
            sp = _dot_nt(keys_ref[hp], q_ref[:, hp * half:(hp + 1) * half], HI)
            s.append(sp)
            tiles = [sp[r * SUBLANES:(r + 1) * SUBLANES, :] for r in range(n_keys // SUBLANES)]
            sv.append(_top_sorted(tiles)[:topk])
        a, b = sv
        b_lo, b_hi = spread(b[:SUBLANES]), spread(b[SUBLANES:])
        a_lo = jnp.where(sub >= 4, spread(a[:SUBLANES]), neg)
        a_hi = spread(a[SUBLANES:])
        cand = [a[0] + b_lo, a[0] + b_hi, a[1] + b_lo, a[2] + b_lo, a[3] + b_lo,
                a_lo + b[0], a_hi + b[0], a_lo + b[1], a_lo + b[2]]
        ts = _top_sorted(cand + [None] * (topk - len(cand)))
        mx = ts[0][0:1, :]
        tot = jnp.exp(ts[0][0:1, :] - mx)
        for r in range(1, topk):
            tot = tot + jnp.exp(ts[r][0:1, :] - mx)
        lse = mx + jnp.log(tot)
        m0 = a[0][0:1, :]
        s0_ref[h] = s[0]
        s1_ref[h] = s[1]
        e0_ref[h] = jnp.exp(s[0] - m0)
        e1_ref[h] = jnp.exp(s[1] - (lse - m0))
        thr_all = jnp.where(hid == h, ts[topk - 1][0:1, :], thr_all)
    thr_ref[...] = thr_all


def _peer_select(q, keys, nh, tb):
    t = q.shape[0]
    n_keys, half = keys.shape[1], keys.shape[2]
    assert PEER_TOPK == 16 and n_keys == 16 * SUBLANES
    big = pl.BlockSpec((nh, n_keys, tb), lambda i: (0, 0, i))
    return pl.pallas_call(
        functools.partial(_peer_select_kernel, nh=nh, half=half, topk=PEER_TOPK),
        grid=(t // tb,),
        in_specs=[pl.BlockSpec((tb, q.shape[1]), lambda i: (i, 0)),
                  pl.BlockSpec(keys.shape, lambda i: (0, 0, 0))],
        out_specs=[big, big, big, big, pl.BlockSpec((nh, tb), lambda i: (0, i))],
        out_shape=[jax.ShapeDtypeStruct((nh, n_keys, t), F32)] * 4 + [jax.ShapeDtypeStruct((nh, t), F32)],
        compiler_params=_params(("parallel",)),
        name="peer_select",
    )(q, keys)


def _peer_dense_kernel(xn_ref, u_ref, vt_ref, s0_ref, s1_ref, e0_ref, e1_ref, thr_ref, res_ref, o_ref,
                       ht_scr, ht2_scr, a_scr, a2_scr, acc_scr, s0x_scr, e0x_scr,
                       *, nh, n_keys, e_blk, t_blk, rows_per_slice):
    e = pl.program_id(1)
    n_i = e_blk // n_keys
    n_sl = n_i // rows_per_slice
    e_sl = rows_per_slice * n_keys
    sqrt_half = float(np.sqrt(0.5))

    @pl.when(e == 0)
    def _():
        acc_scr[...] = jnp.zeros(acc_scr.shape, F32)

    i0 = pl.multiple_of(e * n_i, n_i)
    for h in range(nh):
        srows = s0_ref[h, pl.ds(i0, n_i), :]
        erows = e0_ref[h, pl.ds(i0, n_i), :]
        for ii in range(n_i):
            s0x_scr[ii, h:h + 1, :] = srows[ii:ii + 1, :]
            e0x_scr[ii, h:h + 1, :] = erows[ii:ii + 1, :]

    ht_bufs, a_bufs = (ht_scr, ht2_scr), (a_scr, a2_scr)

    def pre_activations(m):
        ht_bufs[m % 2][...] = _dot_nt(xn_ref[...], u_ref[m * e_sl:(m + 1) * e_sl, :]).T

    def gated_activations(m):
        nv = n_keys // SUBLANES
        for r in range(rows_per_slice):
            ii = m * rows_per_slice + r
            for tc in range(t_blk // LANES):
                ls = slice(tc * LANES, (tc + 1) * LANES)
                gate = [None] * nv
                for h in range(nh):
                    s0r = jnp.broadcast_to(s0x_scr[ii, h:h + 1, ls], (SUBLANES, LANES))
                    e0r = jnp.broadcast_to(e0x_scr[ii, h:h + 1, ls], (SUBLANES, LANES))
                    thr = jnp.broadcast_to(thr_ref[h:h + 1, ls], (SUBLANES, LANES))
                    for v in range(nv):
                        ks = slice(v * SUBLANES, (v + 1) * SUBLANES)
                        ssum = s1_ref[h, ks, ls] + s0r
                        term = jnp.where(ssum >= thr, e1_ref[h, ks, ls], 0.0) * e0r
                        gate[v] = term if gate[v] is None else gate[v] + term
                for v in range(0, nv, 2):
                    rs = slice(r * n_keys + v * SUBLANES, r * n_keys + (v + 2) * SUBLANES)
                    hv = ht_bufs[m % 2][rs, ls]
                    act = 0.5 * hv * (1.0 + lax.erf(hv * sqrt_half))
                    a_bufs[m % 2][rs, ls] = (act * jnp.concatenate(gate[v:v + 2], axis=0)).astype(BF16)

    def accumulate(m):
        acc_scr[...] += _dot(vt_ref[m], a_bufs[m % 2][...])

    pre_activations(0)
    for m in range(n_sl):
        if m >= 1:
            accumulate(m - 1)
        if m + 1 < n_sl:
            pre_activations(m + 1)
        gated_activations(m)
    accumulate(n_sl - 1)

    @pl.when(e == pl.num_programs(1) - 1)
    def _():
        o_ref[...] = res_ref[...] + acc_scr[...].T


def _peer_dense(xn, u, vt, s0t, s1t, e0t, e1t, thr, resid, t_blk, e_blk):
    t, d = xn.shape
    n_exp = u.shape[0]
    nh, n_keys, _ = s0t.shape
    assert e_blk == SUBLANES * n_keys and nh <= SUBLANES
    e_sl = vt.shape[2]
    n_sl = e_blk // e_sl
    big = pl.BlockSpec((nh, n_keys, t_blk), lambda ti, e: (0, 0, ti))
    return pl.pallas_call(
        functools.partial(_peer_dense_kernel, nh=nh, n_keys=n_keys, e_blk=e_blk, t_blk=t_blk,
                          rows_per_slice=e_sl // n_keys),
        grid=(t // t_blk, n_exp // e_blk),
        in_specs=[pl.BlockSpec((t_blk, d), lambda ti, e: (ti, 0)),
                  pl.BlockSpec((e_blk, d), lambda ti, e: (e, 0)),
                  pl.BlockSpec((n_sl, d, e_sl), lambda ti, e: (e, 0, 0)),
                  big, big, big, big,
                  pl.BlockSpec((nh, t_blk), lambda ti, e: (0, ti)),
                  pl.BlockSpec((t_blk, d), lambda ti, e: (ti, 0))],
        out_specs=pl.BlockSpec((t_blk, d), lambda ti, e: (ti, 0)),
        out_shape=jax.ShapeDtypeStruct((t, d), F32),
        scratch_shapes=[pltpu.VMEM((e_sl, t_blk), F32), pltpu.VMEM((e_sl, t_blk), F32),
                        pltpu.VMEM((e_sl, t_blk), BF16), pltpu.VMEM((e_sl, t_blk), BF16),
                        pltpu.VMEM((d, t_blk), F32),
                        pltpu.VMEM((e_blk // n_keys, SUBLANES, t_blk), F32),
                        pltpu.VMEM((e_blk // n_keys, SUBLANES, t_blk), F32)],
        compiler_params=_params(("parallel", "arbitrary")),
        name="peer_dense",
    )(xn, u, vt, s0t, s1t, e0t, e1t, thr, resid)


def _peer(x, gain, wq, keys, u, vt, nh, t_blk):
    q, xn = _mm(x, wq, gain=gain, emit_xn=True)
    s0t, s1t, e0t, e1t, thr = _peer_select(q, keys, nh, LANES)
    return _peer_dense(xn, u, vt, s0t, s1t, e0t, e1t, thr, x, t_blk, SUBLANES * keys.shape[1])


def _rope_tables(pos, rope_dim):
    half = rope_dim // 2
    freq = ROPE_THETA ** (-jnp.arange(half, dtype=F32) / half)
    ang = pos.astype(F32)[:, None] * freq[None, :]
    cos, sin = jnp.cos(ang), jnp.sin(ang)
    z = jnp.zeros_like(cos)
    pad = jnp.zeros((pos.shape[0], LANES - 4 * half), F32)
    cat = lambda *xs: jnp.concatenate(xs + (pad,), axis=1) if pad.shape[1] else jnp.concatenate(xs, axis=1)
    return cat(cos, cos, z, z), cat(-sin, z, z, z), cat(z, sin, z, z)


def _mla_weights(w_in, g_cq, g_ckv, w_uq, g_qn, g_kn, w_uk, w_uv, w_out):
    d = w_in.shape[0]
    q_lora, kv_lora = g_cq.shape[0], g_ckv.shape[0]
    rope_dim = w_in.shape[1] - q_lora - kv_lora
    _, nh, nope = w_uk.shape
    vdim = w_uv.shape[2]
    qk_dim = nope + rope_dim
    assert nope == LANES and 2 * rope_dim <= LANES and rope_dim % 2 == 0 and nh <= SUBLANES
    w_in_p = jnp.concatenate([w_in, jnp.zeros((d, LANES - rope_dim), F32)], axis=1).astype(BF16)
    wq3 = w_uq.reshape(q_lora, nh, qk_dim)
    w_nope = wq3[:, :, :nope].reshape(q_lora, nh * nope)
    w_rope = jnp.pad(wq3[:, :, nope:], ((0, 0), (0, 0), (0, nope - rope_dim))).reshape(q_lora, nh * nope)
    gg = g_qn * g_kn
    hn = nh * nope
    head_of = np.arange(hn) // nope
    seg = np.zeros((hn, LANES), np.float32)
    seg[np.arange(hn), head_of] = 1.0
    seg8 = np.zeros((SUBLANES, hn), np.float32)
    seg8[head_of, np.arange(hn)] = 1.0
    return dict(
        nh=nh, nope=nope, rope=rope_dim, q_lora=q_lora, kv_lora=kv_lora, vdim=vdim,
        w_in=w_in_p,
        gcq=g_cq.reshape(1, -1), gckv=g_ckv.reshape(1, -1),
        wuq=jnp.concatenate([w_nope, w_rope], axis=1).astype(BF16),
        gqn=jnp.tile(gg[:nope], nh).reshape(1, hn),
        gqr=jnp.tile(jnp.pad(gg[nope:], (0, nope - rope_dim)), nh).reshape(1, hn),
        wuk=w_uk.reshape(kv_lora, hn).astype(BF16),
        wuv=w_uv.reshape(kv_lora, nh * vdim).astype(BF16),
        w_out=w_out.astype(BF16),
        seg=jnp.asarray(seg), segt=jnp.asarray(seg.T.copy()), seg8=jnp.asarray(seg8),
    )


def kernel(x_prompt, x_sample, cache_latent, cache_krope, cache_kscale, page_table, state_ssm, state_conv,
           norm_mix, norm_ffn, gdn_w_in, gdn_conv_w, gdn_a_log, gdn_dt_bias, gdn_out_norm, gdn_w_out,
           mla_w_in, mla_cq_norm, mla_ckv_norm, mla_w_uq, mla_q_norm, mla_k_norm, mla_w_uk, mla_w_uv, mla_w_out,
           peer_w_q, peer_sub_keys, peer_u, peer_v):
    bp, seq, d = x_prompt.shape
    bs, dec_seq, _ = x_sample.shape
    assert dec_seq == 1 and bs <= SAMPLE_PAD
    depth = norm_mix.shape[0]
    n_pages, page = page_table.shape[1], cache_latent.shape[2]
    past_len = n_pages * page
    gdn_heads, gdn_dk = state_ssm.shape[2], state_ssm.shape[3]
    gdn_dim = gdn_heads * gdn_dk
    peer_heads, n_keys, peer_half = peer_sub_keys.shape[1], peer_sub_keys.shape[3], peer_sub_keys.shape[4]
    assert n_keys == LANES and peer_half == LANES and gdn_dk == LANES

    tp = bp * seq
    xp = x_prompt.reshape(tp, d)
    xs = jnp.pad(x_sample.reshape(bs, d), ((0, SAMPLE_PAD - bs), (0, 0)))
    gdn_tb = min(512, seq)
    attn_tq = min(512, seq)
    peer_tblk_p = min(512, tp)
    decode_pages = math.gcd(n_pages, 8)

    rope_dim = mla_w_in.shape[2] - mla_cq_norm.shape[1] - mla_ckv_norm.shape[1]
    tabs_p = _rope_tables(jnp.tile(jnp.arange(seq), bp), rope_dim)
    tabs_s = _rope_tables(jnp.full((SAMPLE_PAD,), past_len), rope_dim)

    lat_p, kr_p, ksc_p, ssm_p, conv_p = [], [], [], [], []
    lat_s, kr_s, ksc_s, ssm_s, conv_s = [], [], [], [], []
    for i in range(depth):
        j = i // 2
        if i % 2 == 0:
            w_in = gdn_w_in[j]
            n_extra = w_in.shape[1] - 4 * gdn_dim
            w_in_p = jnp.concatenate([w_in, jnp.zeros((d, LANES - n_extra), F32)], axis=1).astype(BF16)
            w_out = gdn_w_out[j].astype(BF16)
            proj = _mm(xp, w_in_p, gain=norm_mix[i], tn_cap=1408).reshape(bp, seq, -1)
            rows = _gdn_rows(proj[..., 4 * gdn_dim:], gdn_heads, GDN_CHUNK)
            o, sfin = _gdn_mixer(proj, rows, gdn_a_log[j], gdn_dt_bias[j], gdn_conv_w[j], gdn_out_norm[j],
                                 jnp.zeros((bp, CONV_W - 1, 3 * gdn_dim), F32),
                                 jnp.zeros((bp, gdn_heads, gdn_dk, gdn_dk), F32), None, gdn_tb, 1)
            xp = _mm(o.reshape(tp, gdn_dim), w_out, resid=xp)
            ssm_p.append(sfin)
            conv_p.append(proj[:, seq - (CONV_W - 1):, :3 * gdn_dim])
            proj_s = _mm(xs, w_in_p, gain=norm_mix[i], tn_cap=1408)[:bs]
            proj_c = jnp.pad(proj_s[:, None, :], ((0, 0), (0, GDN_CHUNK - 1), (0, 0)))
            rows_s = _gdn_rows(proj_c[..., 4 * gdn_dim:], gdn_heads, GDN_CHUNK)
            o_s, sfin_s = _gdn_mixer(proj_c, rows_s, gdn_a_log[j], gdn_dt_bias[j], gdn_conv_w[j], gdn_out_norm[j],
                                     state_conv[j], state_ssm[j], 1, GDN_CHUNK, gdn_heads)
            o_s = jnp.pad(o_s[:, 0, :], ((0, SAMPLE_PAD - bs), (0, 0)))
            xs = _mm(o_s, w_out, resid=xs)
            ssm_s.append(sfin_s)
            conv_s.append(jnp.concatenate([state_conv[j][:, 1:, :], proj_s[:, None, :3 * gdn_dim]], axis=1))
        else:
            w = _mla_weights(mla_w_in[j], mla_cq_norm[j], mla_ckv_norm[j], mla_w_uq[j], mla_q_norm[j],
                             mla_k_norm[j], mla_w_uk[j], mla_w_uv[j], mla_w_out[j])
            nh, kv_lora, vdim = w["nh"], w["kv_lora"], w["vdim"]
            kc = kv_lora + LANES
            proj = _mm(xp, w["w_in"], gain=norm_mix[i])
            lat, kr, ksc, qcat, kcat, kst = _mla_prep(proj, w, *tabs_p, tm=min(256, tp))
            kst_b = kst.reshape(SUBLANES, bp, seq).transpose(1, 0, 2)
            o = _mla_attention(qcat.reshape(nh, bp, seq, kc), kcat.reshape(bp, seq, kc), kst_b, w["wuv"],
                               nh, kv_lora, vdim, attn_tq)
            xp = _mm(o.reshape(tp, nh * vdim), w["w_out"], resid=xp)
            lat_p.append(lat.reshape(bp, seq, kv_lora))
            kr_p.append(kr.reshape(bp, seq, -1))
            ksc_p.append(ksc.reshape(bp, seq, nh))
            proj_s = _mm(xs, w["w_in"], gain=norm_mix[i])
            lat1, kr1, ksc1, qcat1, kcat1, kst1 = _mla_prep(proj_s, w, *tabs_s, tm=SAMPLE_PAD)
            qd = qcat1[:, :bs, :].transpose(1, 0, 2)
            ksself = kst1[:nh, :bs].T.reshape(bs, nh, 1)
            o_s = _mla_decode(page_table, qd, cache_latent[j], cache_krope[j],
                              cache_kscale[j].transpose(0, 2, 1),
                              kcat1[:bs].reshape(bs, 1, kc), ksself, w["wuv"], nh, kv_lora, w["rope"], vdim,
                              float((w["nope"] + w["rope"]) ** -0.5), decode_pages)
            o_s = jnp.pad(o_s.reshape(bs, nh * vdim), ((0, SAMPLE_PAD - bs), (0, 0)))
            xs = _mm(o_s, w["w_out"], resid=xs)
            lat_s.append(lat1[:bs].reshape(bs, 1, kv_lora))
            kr_s.append(kr1[:bs].reshape(bs, 1, -1))
            ksc_s.append(ksc1[:bs].reshape(bs, 1, nh))
        wq = peer_w_q[i].astype(BF16)
        keys = peer_sub_keys[i].reshape(peer_heads * 2, n_keys, peer_half)
        u = peer_u[i].astype(BF16)
        vt = peer_v[i].astype(BF16).reshape(-1, PEER_SLICE, d).transpose(0, 2, 1)
        xp = _peer(xp, norm_ffn[i], wq, keys, u, vt, peer_heads, peer_tblk_p)
        xs = _peer(xs, norm_ffn[i], wq, keys, u, vt, peer_heads, SAMPLE_PAD)
    return (xp.reshape(bp, seq, d), xs[:bs].reshape(bs, 1, d),
            jnp.stack(lat_p), jnp.stack(kr_p), jnp.stack(ksc_p), jnp.stack(ssm_p), jnp.stack(conv_p),
            jnp.stack(lat_s), jnp.stack(kr_s), jnp.stack(ksc_s), jnp.stack(ssm_s), jnp.stack(conv_s))
```

```python
import functools
import math

import numpy as np
import jax
import jax.numpy as jnp
from jax import lax
from jax.experimental import pallas as pl
from jax.experimental.pallas import tpu as pltpu

F32 = jnp.float32
BF16 = jnp.bfloat16
HI = lax.Precision.HIGHEST

EPS = 1e-6
ROPE_THETA = 10000.0
GDN_CHUNK = 64
CONV_W = 4
PEER_TOPK = 16
LANES = 128
SUBLANES = 8
VMEM_LIMIT = 52 * 1024 * 1024
SAMPLE_PAD = 128
PEER_SLICE = 256

NT_DIMS = (((1,), (1,)), ((), ()))
TN_DIMS = (((0,), (0,)), ((), ()))


def _params(sem):
    return pltpu.CompilerParams(dimension_semantics=sem, vmem_limit_bytes=VMEM_LIMIT)


def _iota(shape, dim):
    return lax.broadcasted_iota(jnp.int32, shape, dim)


def _aligned(x, m):
    return x if isinstance(x, int) else pl.multiple_of(x, m)


def _silu(x):
    return x * jax.nn.sigmoid(x)


def _softplus(x):
    return jnp.maximum(x, 0.0) + jnp.log1p(jnp.exp(-jnp.abs(x)))


def _dot(a, b, precision=None):
    return jnp.dot(a, b, preferred_element_type=F32, precision=precision)


def _dot_nt(a, b, precision=None):
    return lax.dot_general(a, b, NT_DIMS, preferred_element_type=F32, precision=precision)


def _split(x, terms):
    out = []
    for _ in range(terms):
        t = x.astype(BF16)
        out.append(t)
        x = x - t.astype(F32)
    return out


def _dot3(a, b, dot=_dot):
    return dot(a[0], b[0]) + (dot(a[0], b[1]) + dot(a[1], b[0]))


def _dot_exact01(a3, b01, dot=_dot):
    return dot(a3[0], b01) + (dot(a3[1], b01) + dot(a3[2], b01))


def _mm_kernel(*refs, has_gain, has_resid, emit_xn):
    it = iter(refs)
    x_ref, w_ref = next(it), next(it)
    g_ref = next(it) if has_gain else None
    r_ref = next(it) if has_resid else None
    o_ref = next(it)
    xn_ref = next(it) if emit_xn else None
    xb_scr = next(it)

    @pl.when(pl.program_id(1) == 0)
    def _():
        x = x_ref[...].astype(F32)
        if has_gain:
            x = x * lax.rsqrt(jnp.mean(x * x, axis=-1, keepdims=True) + EPS) * g_ref[...]
        xb = x.astype(BF16)
        xb_scr[...] = xb
        if emit_xn:
            xn_ref[...] = xb

    y = _dot(xb_scr[...], w_ref[...])
    if has_resid:
        y = y + r_ref[...]
    o_ref[...] = y.astype(o_ref.dtype)


def _pick_tile(n, cap):
    if n <= cap:
        return n
    best = LANES
    for t in range(LANES, cap + 1, LANES):
        if n % t == 0:
            best = t
    return best


def _mm(x, w, gain=None, resid=None, emit_xn=False, tm_cap=1024, tn_cap=1024):
    m, k = x.shape
    n = w.shape[1]
    tm = _pick_tile(m, tm_cap)
    tn = _pick_tile(n, tn_cap)
    assert m % tm == 0 and n % tn == 0
    in_specs = [pl.BlockSpec((tm, k), lambda i, j: (i, 0)),
                pl.BlockSpec((k, tn), lambda i, j: (0, j))]
    args = [x, w]
    if gain is not None:
        in_specs.append(pl.BlockSpec((1, k), lambda i, j: (0, 0)))
        args.append(gain.reshape(1, k).astype(F32))
    if resid is not None:
        in_specs.append(pl.BlockSpec((tm, tn), lambda i, j: (i, j)))
        args.append(resid)
    out_shape = [jax.ShapeDtypeStruct((m, n), F32)]
    out_specs = [pl.BlockSpec((tm, tn), lambda i, j: (i, j))]
    if emit_xn:
        out_shape.append(jax.ShapeDtypeStruct((m, k), BF16))
        out_specs.append(pl.BlockSpec((tm, k), lambda i, j: (i, 0)))
    res = pl.pallas_call(
        functools.partial(_mm_kernel, has_gain=gain is not None, has_resid=resid is not None, emit_xn=emit_xn),
        grid=(m // tm, n // tn),
        in_specs=in_specs,
        out_specs=out_specs,
        out_shape=out_shape,
        scratch_shapes=[pltpu.VMEM((tm, k), BF16)],
        compiler_params=_params(("parallel", "arbitrary")),
        name="norm_matmul",
    )(*args)
    return res if emit_xn else res[0]


def _unit_lower_inverse(low, eye, ri, ci, chunk):
    base = min(16, chunk)
    assert chunk & (chunk - 1) == 0
    same = lambda size: (ri >> (size.bit_length() - 1)) == (ci >> (size.bit_length() - 1))
    pw = [-jnp.where(same(base), l, 0.0) for l in low]
    inv = [eye + p for p in pw]
    ps = [_split(p, 2) for p in pw]
    for _ in range(max(int(math.ceil(math.log2(base))) - 1, 0)):
        ps = [_split(_dot3(p, p), 2) for p in ps]
        inv = [t + _dot3(_split(t, 2), p) for t, p in zip(inv, ps)]
    size = base
    while size < chunk:
        off = jnp.logical_and(same(2 * size), jnp.logical_not(same(size)))
        ts = [_split(t, 2) for t in inv]
        x = [_dot3(t, _split(jnp.where(off, l, 0.0), 2)) for t, l in zip(ts, low)]
        x = [_dot3(_split(xi, 2), t) for xi, t in zip(x, ts)]
        inv = [t - xi for t, xi in zip(inv, x)]
        size *= 2
    return inv


def _gdn_kernel(q_ref, k_ref, v_ref, z_ref, rows_ref, alog_ref, dt_ref, cwq_ref, cwk_ref, cwv_ref,
                gout_ref, bq_ref, bk_ref, bv_ref, s0_ref, o_ref, sfin_ref,
                xq_scr, xk_scr, xv_scr, s_scr, *, tb, chunk, hb, valid_len):
    t = pl.program_id(2)
    dk = s_scr.shape[-1]
    hist = CONV_W - 1
    base = SUBLANES - hist
    ncb = tb // chunk

    @pl.when(t == 0)
    def _():
        s_scr[...] = s0_ref[0]
        xq_scr[base:SUBLANES, :] = bq_ref[0]
        xk_scr[base:SUBLANES, :] = bk_ref[0]
        xv_scr[base:SUBLANES, :] = bv_ref[0]

    def conv_silu(x_ref, w_ref, scr):
        scr[SUBLANES:SUBLANES + tb, :] = x_ref[0]
        y = w_ref[0:1, :] * scr[base:base + tb, :]
        for j in range(1, CONV_W):
            y = y + w_ref[j:j + 1, :] * scr[base + j:base + j + tb, :]
        scr[base:SUBLANES, :] = scr[tb + base:tb + SUBLANES, :]
        return _silu(y)

    def l2n(x):
        return x * lax.rsqrt(jnp.sum(x * x, axis=-1, keepdims=True) + EPS)

    q_all = conv_silu(q_ref, cwq_ref, xq_scr)
    k_all = conv_silu(k_ref, cwk_ref, xk_scr)
    v_all = conv_silu(v_ref, cwv_ref, xv_scr)

    ri = _iota((chunk, chunk), 0)
    ci = _iota((chunk, chunk), 1)
    eye = (ri == ci).astype(F32)
    incl = ri >= ci
    strict = ri > ci
    eye_b = eye.astype(BF16)
    tril_b = incl.astype(F32).astype(BF16)
    triu_b = (ri <= ci).astype(F32).astype(BF16)
    rid8 = _iota((SUBLANES, chunk), 0)

    units = [(hh, c) for c in range(ncb) for hh in range(hb)]
    qs, ks, vs, beta, g_col, decay, low = [], [], [], [], [], [], []
    for hh, c in units:
        hs = slice(hh * dk, (hh + 1) * dk)
        r0 = c * chunk
        qs.append(l2n(q_all[r0:r0 + chunk, hs]) * (dk ** -0.5))
        ks.append(l2n(k_all[r0:r0 + chunk, hs]))
        vs.append(v_all[r0:r0 + chunk, hs])
        raw = rows_ref[0, hh, c]
        m8 = jnp.where(rid8 == 0, jax.nn.sigmoid(raw),
                       jnp.where(rid8 == 1, -jnp.exp(alog_ref[hh]) * _softplus(raw + dt_ref[hh]), 0.0))
        if valid_len is not None:
            pos = t * tb + r0 + _iota((SUBLANES, chunk), 1)
            m8 = jnp.where(pos < valid_len, m8, 0.0)
        m3 = _split(m8, 3)
        flip = lambda a, b: _dot_nt(b, a)
        g_row = _dot_exact01(m3, triu_b)[1:2, :]
        beta.append(_dot_exact01(m3, eye_b, flip)[:, 0:1])
        g_col.append(_dot_exact01(m3, tril_b, flip)[:, 1:2])
        decay.append(jnp.exp(jnp.where(incl, g_col[-1] - g_row, -jnp.inf)))
    kb = [x.astype(BF16) for x in ks]
    for n in range(len(units)):
        low.append(jnp.where(strict, beta[n] * _dot_nt(kb[n], kb[n]) * decay[n], 0.0))
    inv = _unit_lower_inverse(low, eye, ri, ci, chunk)
    e_g = [jnp.exp(g) for g in g_col]
    w = [_dot3(_split(inv[n], 2),
               _split(jnp.concatenate([vs[n] * beta[n], ks[n] * (beta[n] * e_g[n])], axis=1), 2))
         for n in range(len(units))]
    qk = [_dot_nt(qs[n].astype(BF16), kb[n]) * decay[n] for n in range(len(units))]

    for n, (hh, c) in enumerate(units):
        hs = slice(hh * dk, (hh + 1) * dk)
        r0 = c * chunk
        g_last = g_col[n][chunk - 1:chunk, :]
        q_dec = qs[n] * e_g[n]
        k_dec = ks[n] * jnp.exp(g_last - g_col[n])
        s = s_scr[hh]
        sb = s.astype(BF16)
        u = w[n][:, :dk] - _dot(w[n][:, dk:].astype(BF16), sb)
        ub = u.astype(BF16)
        o = _dot(q_dec.astype(BF16), sb) + _dot(qk[n].astype(BF16), ub)
        s_scr[hh] = s * jnp.exp(g_last) + lax.dot_general(k_dec.astype(BF16), ub, TN_DIMS,
                                                           preferred_element_type=F32)
        on = o * lax.rsqrt(jnp.mean(o * o, axis=-1, keepdims=True) + EPS) * gout_ref[...]
        o_ref[0, r0:r0 + chunk, hs] = on * _silu(z_ref[0, r0:r0 + chunk, hs])

    @pl.when(t == pl.num_programs(2) - 1)
    def _():
        sfin_ref[0] = s_scr[...]


def _gdn_mixer(proj, rows, a_log, dt_bias, conv_w, g_out, buf0, s0, valid_len, tb, hb):
    b, t, _ = proj.shape
    nh = s0.shape[1]
    dk = s0.shape[2]
    chunk = rows.shape[-1]
    dim = nh * dk
    ng = nh // hb
    wdt = hb * dk
    col = lambda off: pl.BlockSpec((1, tb, wdt), lambda bi, h, ti, off=off: (bi, ti, off + h))
    cw = lambda off: pl.BlockSpec((CONV_W, wdt), lambda bi, h, ti, off=off: (0, off + h))
    bf = lambda off: pl.BlockSpec((1, CONV_W - 1, wdt), lambda bi, h, ti, off=off: (bi, 0, off + h))
    hscal = pl.BlockSpec((hb, 1, 1), lambda bi, h, ti: (h, 0, 0))
    o, sfin = pl.pallas_call(
        functools.partial(_gdn_kernel, tb=tb, chunk=chunk, hb=hb, valid_len=valid_len),
        grid=(b, ng, t // tb),
        in_specs=[col(0), col(ng), col(2 * ng), col(3 * ng),
                  pl.BlockSpec((1, hb, tb // chunk, SUBLANES, chunk), lambda bi, h, ti: (bi, h, ti, 0, 0)),
                  hscal, hscal, cw(0), cw(ng), cw(2 * ng),
                  pl.BlockSpec((1, dk), lambda bi, h, ti: (0, 0)),
                  bf(0), bf(ng), bf(2 * ng),
                  pl.BlockSpec((1, hb, dk, dk), lambda bi, h, ti: (bi, h, 0, 0))],
        out_specs=[pl.BlockSpec((1, tb, wdt), lambda bi, h, ti: (bi, ti, h)),
                   pl.BlockSpec((1, hb, dk, dk), lambda bi, h, ti: (bi, h, 0, 0))],
        out_shape=[jax.ShapeDtypeStruct((b, t, dim), F32),
                   jax.ShapeDtypeStruct((b, nh, dk, dk), F32)],
        scratch_shapes=[pltpu.VMEM((tb + SUBLANES, wdt), F32)] * 3 + [pltpu.VMEM((hb, dk, dk), F32)],
        compiler_params=_params(("parallel", "parallel", "arbitrary")),
        name="gdn_chunked",
    )(proj, proj, proj, proj, rows, a_log.reshape(nh, 1, 1), dt_bias.reshape(nh, 1, 1),
      conv_w, conv_w, conv_w, g_out.reshape(1, dk), buf0, buf0, buf0, s0)
    return o, sfin


def _gdn_rows(ba, nh, chunk):
    b, t, _ = ba.shape
    x = jnp.stack([ba[..., :nh], ba[..., nh:2 * nh]], axis=-1)
    x = x.reshape(b, t // chunk, chunk, nh, 2).transpose(0, 3, 1, 4, 2)
    return jnp.pad(x, ((0, 0),) * 3 + ((0, SUBLANES - 2), (0, 0)))


def _rope_block(x, cos_t, sin_a, sin_b):
    q4 = x.shape[-1] // 4
    return x * cos_t + pltpu.roll(x, 3 * q4, 1) * sin_a + pltpu.roll(x, q4, 1) * sin_b


def _mla_prep_kernel(p_ref, gcq_ref, gckv_ref, wuq_ref, gqn_ref, gqr_ref, wuk_ref, seg_ref, segt_ref,
                     seg8_ref, cos_ref, sina_ref, sinb_ref,
                     lat_ref, kr_ref, ksc_ref, qcat_ref, kcat_ref, kst_ref,
                     *, q_lora, kv_lora, nh, nope, rope_dim, scale):
    qk_dim = nope + rope_dim
    hn = nh * nope
    x = p_ref[...]
    cq = x[:, :q_lora]
    cq = cq * lax.rsqrt(jnp.mean(cq * cq, axis=-1, keepdims=True) + EPS) * gcq_ref[...]
    ckv = x[:, q_lora:q_lora + kv_lora]
    ckv = ckv * lax.rsqrt(jnp.mean(ckv * ckv, axis=-1, keepdims=True) + EPS) * gckv_ref[...]
    cos_t, sin_a, sin_b = cos_ref[...], sina_ref[...], sinb_ref[...]
    krb = _rope_block(x[:, q_lora + kv_lora:], cos_t, sin_a, sin_b)
    lat_ref[...] = ckv
    kr_ref[...] = krb[:, :rope_dim]
    ckv_b = ckv.astype(BF16)
    kcat_ref[:, :kv_lora] = ckv_b
    kcat_ref[:, kv_lora:] = krb.astype(BF16)

    qf = _dot(cq.astype(BF16), wuq_ref[...])
    qn = qf[:, :hn]
    qr = jnp.concatenate([_rope_block(qf[:, hn + h * nope:hn + (h + 1) * nope], cos_t, sin_a, sin_b)
                          for h in range(nh)], axis=1)
    ssq = _dot(qn * qn + qr * qr, seg_ref[...], HI)
    inv = lax.rsqrt(ssq * (1.0 / qk_dim) + EPS)
    inv_x = _dot(inv, segt_ref[...], HI)
    qn = qn * inv_x * gqn_ref[...]
    qr = qr * inv_x * gqr_ref[...]

    k_nope = _dot(ckv_b, wuk_ref[...])
    k2 = k_nope * k_nope
    kr2 = krb * krb
    ksq = _dot(k2, seg_ref[...], HI) + jnp.sum(kr2, axis=-1, keepdims=True)
    ksc = lax.rsqrt(ksq * (1.0 / qk_dim) + EPS)
    ksc_ref[...] = ksc[:, :nh]
    ksq_t = _dot_nt(seg8_ref[...], k2, HI) + _dot_nt(jnp.ones((SUBLANES, kr2.shape[1]), F32), kr2, HI)
    kst_ref[...] = lax.rsqrt(ksq_t * (1.0 / qk_dim) + EPS) * scale

    for h in range(nh):
        sl = slice(h * nope, (h + 1) * nope)
        q_lat = _dot_nt(qn[:, sl].astype(BF16), wuk_ref[:, sl])
        qcat_ref[h, :, :kv_lora] = q_lat.astype(BF16)
        qcat_ref[h, :, kv_lora:] = qr[:, sl].astype(BF16)


def _mla_prep(proj, w, cos_t, sin_a, sin_b, tm):
    m = proj.shape[0]
    nh, nope, rope_dim, q_lora, kv_lora = w["nh"], w["nope"], w["rope"], w["q_lora"], w["kv_lora"]
    kc = kv_lora + LANES
    full = lambda a: pl.BlockSpec(a.shape, lambda i: (0,) * a.ndim)
    rows = lambda n: pl.BlockSpec((tm, n), lambda i: (i, 0))
    consts = [w["gcq"], w["gckv"], w["wuq"], w["gqn"], w["gqr"], w["wuk"], w["seg"], w["segt"], w["seg8"]]
    return pl.pallas_call(
        functools.partial(_mla_prep_kernel, q_lora=q_lora, kv_lora=kv_lora, nh=nh, nope=nope,
                          rope_dim=rope_dim, scale=float((nope + rope_dim) ** -0.5)),
        grid=(m // tm,),
        in_specs=[rows(proj.shape[1])] + [full(a) for a in consts] + [rows(LANES)] * 3,
        out_specs=[rows(kv_lora), rows(rope_dim), rows(nh),
                   pl.BlockSpec((nh, tm, kc), lambda i: (0, i, 0)), rows(kc),
                   pl.BlockSpec((SUBLANES, tm), lambda i: (0, i))],
        out_shape=[jax.ShapeDtypeStruct((m, kv_lora), F32), jax.ShapeDtypeStruct((m, rope_dim), F32),
                   jax.ShapeDtypeStruct((m, nh), F32), jax.ShapeDtypeStruct((nh, m, kc), BF16),
                   jax.ShapeDtypeStruct((m, kc), BF16), jax.ShapeDtypeStruct((SUBLANES, m), F32)],
        compiler_params=_params(("parallel",)),
        name="mla_prep",
    )(proj, *consts, cos_t, sin_a, sin_b)


def _attn_kernel(q_ref, k_ref, ks_ref, wuv_ref, o_ref, m_scr, l_scr, acc_scr, *, nh, kv_lora, vdim, tq, tk):
    qi, kj = pl.program_id(1), pl.program_id(2)

    @pl.when(kj == 0)
    def _():
        m_scr[...] = jnp.full(m_scr.shape, -jnp.inf, F32)
        l_scr[...] = jnp.zeros(l_scr.shape, F32)
        acc_scr[...] = jnp.zeros(acc_scr.shape, F32)

    @pl.when(kj <= qi)
    def _():
        kblk = k_ref[0]
        vblk = kblk[:, :kv_lora]
        causal = (kj * tk + _iota((tq, tk), 1)) <= (qi * tq + _iota((tq, tk), 0))
        for h in range(nh):
            s = _dot_nt(q_ref[h, 0], kblk) * ks_ref[0, h:h + 1, :]
            s = jnp.where(causal, s, -jnp.inf)
            m_old = m_scr[h]
            m_new = jnp.maximum(m_old, jnp.max(s, axis=-1, keepdims=True))
            alpha = jnp.exp(m_old - m_new)
            p = jnp.exp(s - m_new)
            l_scr[h] = alpha * l_scr[h] + jnp.sum(p, axis=-1, keepdims=True)
            acc_scr[h] = alpha * acc_scr[h] + _dot(p.astype(BF16), vblk)
            m_scr[h] = m_new

    @pl.when(kj == qi)
    def _():
        for h in range(nh):
            o_lat = acc_scr[h] / l_scr[h]
            o_ref[0, :, h * vdim:(h + 1) * vdim] = _dot(o_lat.astype(BF16), wuv_ref[:, h * vdim:(h + 1) * vdim])


def _mla_attention(qcat, kcat, kst, wuv, nh, kv_lora, vdim, tq):
    _, b, t, kc = qcat.shape
    tk = tq
    return pl.pallas_call(
        functools.partial(_attn_kernel, nh=nh, kv_lora=kv_lora, vdim=vdim, tq=tq, tk=tk),
        grid=(b, t // tq, t // tk),
        in_specs=[pl.BlockSpec((nh, 1, tq, kc), lambda bi, qi, kj: (0, bi, qi, 0)),
                  pl.BlockSpec((1, tk, kc), lambda bi, qi, kj: (bi, jnp.minimum(kj, qi), 0)),
                  pl.BlockSpec((1, SUBLANES, tk), lambda bi, qi, kj: (bi, 0, jnp.minimum(kj, qi))),
                  pl.BlockSpec(wuv.shape, lambda bi, qi, kj: (0, 0))],
        out_specs=pl.BlockSpec((1, tq, nh * vdim), lambda bi, qi, kj: (bi, qi, 0)),
        out_shape=jax.ShapeDtypeStruct((b, t, nh * vdim), F32),
        scratch_shapes=[pltpu.VMEM((nh, tq, 1), F32), pltpu.VMEM((nh, tq, 1), F32),
                        pltpu.VMEM((nh, tq, kv_lora), F32)],
        compiler_params=_params(("parallel", "parallel", "arbitrary")),
        name="mla_prefill_attention",
    )(qcat, kcat, kst, wuv)


def _decode_kernel(pt_ref, q_ref, *refs, nh, kv_lora, rope_dim, vdim, scale, npg):
    lat_refs, kr_refs, ks_refs = refs[:npg], refs[npg:2 * npg], refs[2 * npg:3 * npg]
    kself_ref, ksself_ref, wuv_ref, o_ref, m_scr, l_scr, acc_scr = refs[3 * npg:]
    p = pl.program_id(1)

    @pl.when(p == 0)
    def _():
        m_scr[...] = jnp.full(m_scr.shape, -jnp.inf, F32)
        l_scr[...] = jnp.zeros(l_scr.shape, F32)
        acc_scr[...] = jnp.zeros(acc_scr.shape, F32)

    q = q_ref[0]
    q_lat, q_rope = q[:, :kv_lora], q[:, kv_lora:kv_lora + rope_dim]
    lat = [r[0, 0].astype(BF16) for r in lat_refs]
    s = jnp.concatenate(
        [(_dot_nt(q_lat, lat[g]) + _dot_nt(q_rope, kr_refs[g][0, 0].astype(BF16))) * (ks_refs[g][0, 0] * scale)
         for g in range(npg)], axis=1)
    m_old = m_scr[...]
    m_new = jnp.maximum(m_old, jnp.max(s, axis=-1, keepdims=True))
    alpha = jnp.exp(m_old - m_new)
    pr = jnp.exp(s - m_new)
    l_scr[...] = alpha * l_scr[...] + jnp.sum(pr, axis=-1, keepdims=True)
    page = lat[0].shape[0]
    pv = _dot(pr[:, :page].astype(BF16), lat[0])
    for g in range(1, npg):
        pv = pv + _dot(pr[:, g * page:(g + 1) * page].astype(BF16), lat[g])
    acc_scr[...] = alpha * acc_scr[...] + pv
    m_scr[...] = m_new

    @pl.when(p == pl.num_programs(1) - 1)
    def _():
        kself = kself_ref[0].astype(F32)
        s_self = jnp.sum(q.astype(F32) * kself, axis=-1, keepdims=True) * ksself_ref[0]
        m_old = m_scr[...]
        m_new = jnp.maximum(m_old, s_self)
        alpha = jnp.exp(m_old - m_new)
        p_self = jnp.exp(s_self - m_new)
        l_fin = alpha * l_scr[...] + p_self
        acc = alpha * acc_scr[...] + p_self.astype(BF16).astype(F32) * kself[:, :kv_lora]
        o_all = _dot((acc / l_fin).astype(BF16), wuv_ref[...])
        for h in range(nh):
            o_ref[0, :, h * vdim:(h + 1) * vdim] = o_all[h:h + 1, h * vdim:(h + 1) * vdim]


def _mla_decode(page_table, qd, cache_lat, cache_kr, cache_kst, kself, ksself, wuv, nh, kv_lora, rope_dim, vdim,
                scale, npg, layer):
    b, n_pages = page_table.shape
    page = cache_lat.shape[2]
    kc = qd.shape[-1]
    assert n_pages % npg == 0
    pg = lambda shape, g: pl.BlockSpec(shape, lambda bi, p, pt, g=g: (layer, pt[bi, p * npg + g], 0, 0))
    grid_spec = pltpu.PrefetchScalarGridSpec(
        num_scalar_prefetch=1,
        grid=(b, n_pages // npg),
        in_specs=([pl.BlockSpec((1, nh, kc), lambda bi, p, pt: (bi, 0, 0))]
                  + [pg((1, 1, page, kv_lora), g) for g in range(npg)]
                  + [pg((1, 1, page, rope_dim), g) for g in range(npg)]
                  + [pg((1, 1, nh, page), g) for g in range(npg)]
                  + [pl.BlockSpec((1, 1, kc), lambda bi, p, pt: (bi, 0, 0)),
                     pl.BlockSpec((1, nh, 1), lambda bi, p, pt: (bi, 0, 0)),
                     pl.BlockSpec(wuv.shape, lambda bi, p, pt: (0, 0))]),
        out_specs=pl.BlockSpec((1, 1, nh * vdim), lambda bi, p, pt: (bi, 0, 0)),
        scratch_shapes=[pltpu.VMEM((nh, 1), F32), pltpu.VMEM((nh, 1), F32), pltpu.VMEM((nh, kv_lora), F32)],
    )
    return pl.pallas_call(
        functools.partial(_decode_kernel, nh=nh, kv_lora=kv_lora, rope_dim=rope_dim, vdim=vdim, scale=scale,
                          npg=npg),
        grid_spec=grid_spec,
        out_shape=jax.ShapeDtypeStruct((b, 1, nh * vdim), F32),
        compiler_params=_params(("parallel", "arbitrary")),
        name="mla_paged_decode",
    )(page_table, qd, *([cache_lat] * npg), *([cache_kr] * npg), *([cache_kst] * npg), kself, ksself, wuv)


def _oddeven_merge_sort_pairs(n):
    pairs = []

    def merge(lo, m, r):
        step = r * 2
        if step < m:
            merge(lo, m, step)
            merge(lo + r, m, step)
            pairs.extend((i, i + r) for i in range(lo + r, lo + m - r, step))
        else:
            pairs.append((lo, lo + r))

    def sort(lo, m):
        if m > 1:
            sort(lo, m // 2)
            sort(lo + m // 2, m // 2)
            merge(lo, m, 1)

    sort(0, n)
    return pairs


def _cmp_exchange(v, i, j):
    a, b = v[i], v[j]
    if b is None:
        return
    if a is None:
        v[i], v[j] = b, None
        return
    v[i], v[j] = jnp.maximum(a, b), jnp.minimum(a, b)


def _top_sorted(v):
    k = len(v)
    v = list(v)
    for i, j in _oddeven_merge_sort_pairs(k):
        _cmp_exchange(v, i, j)
    shift = SUBLANES // 2
    while shift >= 1:
        rot = [None if x is None else pltpu.roll(x, shift, 0) for x in v]
        v = [rot[k - 1 - i] if v[i] is None else (v[i] if rot[k - 1 - i] is None
                                                   else jnp.maximum(v[i], rot[k - 1 - i])) for i in range(k)]
        d = k // 2
        while d >= 1:
            for i in range(k):
                if i & d == 0:
                    _cmp_exchange(v, i, i + d)
            d //= 2
        shift //= 2
    return v


def _peer_select_kernel(q_ref, keys_ref, s0_ref, s1_ref, e0_ref, e1_ref, thr_ref, *, nh, half, topk):
    tb = q_ref.shape[0]
    n_keys = keys_ref.shape[1]
    sub = _iota((SUBLANES, tb), 0)
    hid = _iota((nh, tb), 0)
    neg = jnp.full((SUBLANES, tb), -jnp.inf, F32)

    def spread(vals):
        out = vals[SUBLANES - 1]
        for r in range(SUBLANES - 2, -1, -1):
            out = jnp.where(sub == r, vals[r], out)
        return out

    thr_all = jnp.zeros((nh, tb), F32)
    for h in range(nh):
        s, sv = [], []
        for p in range(2):
            hp = 2 * h + p
            sp = _dot_nt(keys_ref[hp], q_ref[:, hp * half:(hp + 1) * half], HI)
            s.append(sp)
            tiles = [sp[r * SUBLANES:(r + 1) * SUBLANES, :] for r in range(n_keys // SUBLANES)]
            sv.append(_top_sorted(tiles)[:topk])
        a, b = sv
        b_lo, b_hi = spread(b[:SUBLANES]), spread(b[SUBLANES:])
        a_lo = jnp.where(sub >= 4, spread(a[:SUBLANES]), neg)
        a_hi = spread(a[SUBLANES:])
        cand = [a[0] + b_lo, a[0] + b_hi, a[1] + b_lo, a[2] + b_lo, a[3] + b_lo,
                a_lo + b[0], a_hi + b[0], a_lo + b[1], a_lo + b[2]]
        ts = _top_sorted(cand + [None] * (topk - len(cand)))
        mx = ts[0][0:1, :]
        tot = jnp.exp(ts[0][0:1, :] - mx)
        for r in range(1, topk):
            tot = tot + jnp.exp(ts[r][0:1, :] - mx)
        lse = mx + jnp.log(tot)
        m0 = a[0][0:1, :]
        s0_ref[h] = s[0]
        s1_ref[h] = s[1]
        e0_ref[h] = jnp.exp(s[0] - m0)
        e1_ref[h] = jnp.exp(s[1] - (lse - m0))
        thr_all = jnp.where(hid == h, ts[topk - 1][0:1, :], thr_all)
    thr_ref[...] = thr_all


def _peer_select(q, keys, nh, tb):
    t = q.shape[0]
    n_keys, half = keys.shape[1], keys.shape[2]
    assert PEER_TOPK == 16 and n_keys == 16 * SUBLANES
    big = pl.BlockSpec((nh, n_keys, tb), lambda i: (0, 0, i))
    return pl.pallas_call(
        functools.partial(_peer_select_kernel, nh=nh, half=half, topk=PEER_TOPK),
        grid=(t // tb,),
        in_specs=[pl.BlockSpec((tb, q.shape[1]), lambda i: (i, 0)),
                  pl.BlockSpec(keys.shape, lambda i: (0, 0, 0))],
        out_specs=[big, big, big, big, pl.BlockSpec((nh, tb), lambda i: (0, i))],
        out_shape=[jax.ShapeDtypeStruct((nh, n_keys, t), F32)] * 4 + [jax.ShapeDtypeStruct((nh, t), F32)],
        compiler_params=_params(("parallel",)),
        name="peer_select",
    )(q, keys)


def _peer_dense_kernel(xn_ref, u_ref, unext_ref, vt_ref, vtprev_ref, s0_ref, s1_ref, e0_ref, e1_ref, thr_ref,
                       res_ref, o_ref, ht_scr, ht2_scr, a_scr, a2_scr, acc_scr, s0x_scr, e0x_scr,
                       *, nh, n_keys, e_blk, t_blk, rows_per_slice):
    e = pl.program_id(1)
    n_i = e_blk // n_keys
    n_sl = n_i // rows_per_slice
    e_sl = rows_per_slice * n_keys
    sqrt_half = float(np.sqrt(0.5))
    ht_bufs, a_bufs = (ht_scr, ht2_scr), (a_scr, a2_scr)
    last = (n_sl - 1) % 2

    @pl.when(e == 0)
    def _():
        acc_scr[...] = jnp.zeros(acc_scr.shape, F32)
        a_bufs[last][...] = jnp.zeros(a_bufs[last].shape, BF16)
        ht_bufs[0][...] = _dot_nt(xn_ref[...], u_ref[0:e_sl, :]).T

    i0 = pl.multiple_of(e * n_i, n_i)
    for h in range(nh):
        srows = s0_ref[h, pl.ds(i0, n_i), :]
        erows = e0_ref[h, pl.ds(i0, n_i), :]
        for ii in range(n_i):
            s0x_scr[ii, h:h + 1, :] = srows[ii:ii + 1, :]
            e0x_scr[ii, h:h + 1, :] = erows[ii:ii + 1, :]

    def pre_activations(m):
        u_sl = unext_ref[...] if m == n_sl else u_ref[m * e_sl:(m + 1) * e_sl, :]
        ht_bufs[m % 2][...] = _dot_nt(xn_ref[...], u_sl).T

    def gated_activations(m):
        nv = n_keys // SUBLANES
        for r in range(rows_per_slice):
            ii = m * rows_per_slice + r
            for tc in range(t_blk // LANES):
                ls = slice(tc * LANES, (tc + 1) * LANES)
                gate = [None] * nv
                for h in range(nh):
                    s0r = jnp.broadcast_to(s0x_scr[ii, h:h + 1, ls], (SUBLANES, LANES))
                    e0r = jnp.broadcast_to(e0x_scr[ii, h:h + 1, ls], (SUBLANES, LANES))
                    thr = jnp.broadcast_to(thr_ref[h:h + 1, ls], (SUBLANES, LANES))
                    for v in range(nv):
                        ks = slice(v * SUBLANES, (v + 1) * SUBLANES)
                        ssum = s1_ref[h, ks, ls] + s0r
                        term = jnp.where(ssum >= thr, e1_ref[h, ks, ls], 0.0) * e0r
                        gate[v] = term if gate[v] is None else gate[v] + term
                for v in range(0, nv, 2):
                    rs = slice(r * n_keys + v * SUBLANES, r * n_keys + (v + 2) * SUBLANES)
                    hv = ht_bufs[m % 2][rs, ls]
                    act = 0.5 * hv * (1.0 + lax.erf(hv * sqrt_half))
                    a_bufs[m % 2][rs, ls] = (act * jnp.concatenate(gate[v:v + 2], axis=0)).astype(BF16)

    def accumulate(m):
        v_sl = vtprev_ref[0] if m == -1 else vt_ref[m]
        acc_scr[...] += _dot(v_sl, a_bufs[m % 2][...])

    assert n_sl % 2 == 0
    for m in range(n_sl):
        accumulate(m - 1)
        pre_activations(m + 1)
        gated_activations(m)

    @pl.when(e == pl.num_programs(1) - 1)
    def _():
        accumulate(n_sl - 1)
        o_ref[...] = res_ref[...] + acc_scr[...].T


def _peer_dense(xn, u, vt, s0t, s1t, e0t, e1t, thr, resid, t_blk, e_blk):
    t, d = xn.shape
    n_exp = u.shape[0]
    nh, n_keys, _ = s0t.shape
    assert e_blk == SUBLANES * n_keys and nh <= SUBLANES
    e_sl = vt.shape[2]
    n_sl = e_blk // e_sl
    ne = n_exp // e_blk
    big = pl.BlockSpec((nh, n_keys, t_blk), lambda ti, e: (0, 0, ti))
    return pl.pallas_call(
        functools.partial(_peer_dense_kernel, nh=nh, n_keys=n_keys, e_blk=e_blk, t_blk=t_blk,
                          rows_per_slice=e_sl // n_keys),
        grid=(t // t_blk, ne),
        in_specs=[pl.BlockSpec((t_blk, d), lambda ti, e: (ti, 0)),
                  pl.BlockSpec((e_blk, d), lambda ti, e: (e, 0)),
                  pl.BlockSpec((e_sl, d), lambda ti, e: (jnp.minimum(e + 1, ne - 1) * n_sl, 0)),
                  pl.BlockSpec((n_sl, d, e_sl), lambda ti, e: (e, 0, 0)),
                  pl.BlockSpec((1, d, e_sl), lambda ti, e: (jnp.maximum(e * n_sl - 1, 0), 0, 0)),
                  big, big, big, big,
                  pl.BlockSpec((nh, t_blk), lambda ti, e: (0, ti)),
                  pl.BlockSpec((t_blk, d), lambda ti, e: (ti, 0))],
        out_specs=pl.BlockSpec((t_blk, d), lambda ti, e: (ti, 0)),
        out_shape=jax.ShapeDtypeStruct((t, d), F32),
        scratch_shapes=[pltpu.VMEM((e_sl, t_blk), F32), pltpu.VMEM((e_sl, t_blk), F32),
                        pltpu.VMEM((e_sl, t_blk), BF16), pltpu.VMEM((e_sl, t_blk), BF16),
                        pltpu.VMEM((d, t_blk), F32),
                        pltpu.VMEM((e_blk // n_keys, SUBLANES, t_blk), F32),
                        pltpu.VMEM((e_blk // n_keys, SUBLANES, t_blk), F32)],
        compiler_params=_params(("parallel", "arbitrary")),
        name="peer_dense",
    )(xn, u, u, vt, vt, s0t, s1t, e0t, e1t, thr, resid)


def _peer(x, gain, wq, keys, u, vt, nh, t_blk):
    q, xn = _mm(x, wq, gain=gain, emit_xn=True)
    s0t, s1t, e0t, e1t, thr = _peer_select(q, keys, nh, LANES)
    return _peer_dense(xn, u, vt, s0t, s1t, e0t, e1t, thr, x, t_blk, SUBLANES * keys.shape[1])


def _rope_tables(pos, rope_dim):
    half = rope_dim // 2
    freq = ROPE_THETA ** (-jnp.arange(half, dtype=F32) / half)
    ang = pos.astype(F32)[:, None] * freq[None, :]
    cos, sin = jnp.cos(ang), jnp.sin(ang)
    z = jnp.zeros_like(cos)
    pad = jnp.zeros((pos.shape[0], LANES - 4 * half), F32)
    cat = lambda *xs: jnp.concatenate(xs + (pad,), axis=1) if pad.shape[1] else jnp.concatenate(xs, axis=1)
    return cat(cos, cos, z, z), cat(-sin, z, z, z), cat(z, sin, z, z)


def _mla_weights(w_in, g_cq, g_ckv, w_uq, g_qn, g_kn, w_uk, w_uv, w_out):
    d = w_in.shape[0]
    q_lora, kv_lora = g_cq.shape[0], g_ckv.shape[0]
    rope_dim = w_in.shape[1] - q_lora - kv_lora
    _, nh, nope = w_uk.shape
    vdim = w_uv.shape[2]
    qk_dim = nope + rope_dim
    assert nope == LANES and 2 * rope_dim <= LANES and rope_dim % 2 == 0 and nh <= SUBLANES
    w_in_p = jnp.concatenate([w_in, jnp.zeros((d, LANES - rope_dim), F32)], axis=1).astype(BF16)
    wq3 = w_uq.reshape(q_lora, nh, qk_dim)
    w_nope = wq3[:, :, :nope].reshape(q_lora, nh * nope)
    w_rope = jnp.pad(wq3[:, :, nope:], ((0, 0), (0, 0), (0, nope - rope_dim))).reshape(q_lora, nh * nope)
    gg = g_qn * g_kn
    hn = nh * nope
    head_of = np.arange(hn) // nope
    seg = np.zeros((hn, LANES), np.float32)
    seg[np.arange(hn), head_of] = 1.0
    seg8 = np.zeros((SUBLANES, hn), np.float32)
    seg8[head_of, np.arange(hn)] = 1.0
    return dict(
        nh=nh, nope=nope, rope=rope_dim, q_lora=q_lora, kv_lora=kv_lora, vdim=vdim,
        w_in=w_in_p,
        gcq=g_cq.reshape(1, -1), gckv=g_ckv.reshape(1, -1),
        wuq=jnp.concatenate([w_nope, w_rope], axis=1).astype(BF16),
        gqn=jnp.tile(gg[:nope], nh).reshape(1, hn),
        gqr=jnp.tile(jnp.pad(gg[nope:], (0, nope - rope_dim)), nh).reshape(1, hn),
        wuk=w_uk.reshape(kv_lora, hn).astype(BF16),
        wuv=w_uv.reshape(kv_lora, nh * vdim).astype(BF16),
        w_out=w_out.astype(BF16),
        seg=jnp.asarray(seg), segt=jnp.asarray(seg.T.copy()), seg8=jnp.asarray(seg8),
    )


def kernel(x_prompt, x_sample, cache_latent, cache_krope, cache_kscale, page_table, state_ssm, state_conv,
           norm_mix, norm_ffn, gdn_w_in, gdn_conv_w, gdn_a_log, gdn_dt_bias, gdn_out_norm, gdn_w_out,
           mla_w_in, mla_cq_norm, mla_ckv_norm, mla_w_uq, mla_q_norm, mla_k_norm, mla_w_uk, mla_w_uv, mla_w_out,
           peer_w_q, peer_sub_keys, peer_u, peer_v):
    bp, seq, d = x_prompt.shape
    bs, dec_seq, _ = x_sample.shape
    assert dec_seq == 1 and bs <= SAMPLE_PAD
    depth = norm_mix.shape[0]
    n_pages, page = page_table.shape[1], cache_latent.shape[2]
    past_len = n_pages * page
    gdn_heads, gdn_dk = state_ssm.shape[2], state_ssm.shape[3]
    gdn_dim = gdn_heads * gdn_dk
    peer_heads, n_keys, peer_half = peer_sub_keys.shape[1], peer_sub_keys.shape[3], peer_sub_keys.shape[4]
    assert n_keys == LANES and peer_half == LANES and gdn_dk == LANES

    tp = bp * seq
    xp = x_prompt.reshape(tp, d)
    xs = jnp.pad(x_sample.reshape(bs, d), ((0, SAMPLE_PAD - bs), (0, 0)))
    gdn_tb = min(512, seq)
    attn_tq = min(512, seq)
    peer_tblk_p = min(512, tp)
    decode_pages = math.gcd(n_pages, 8)

    rope_dim = mla_w_in.shape[2] - mla_cq_norm.shape[1] - mla_ckv_norm.shape[1]
    tabs_p = _rope_tables(jnp.tile(jnp.arange(seq), bp), rope_dim)
    tabs_s = _rope_tables(jnp.full((SAMPLE_PAD,), past_len), rope_dim)
    cache_kscale_t = cache_kscale.transpose(0, 1, 3, 2)

    lat_p, kr_p, ksc_p, ssm_p, conv_p = [], [], [], [], []
    lat_s, kr_s, ksc_s, ssm_s, conv_s = [], [], [], [], []
    for i in range(depth):
        j = i // 2
        if i % 2 == 0:
            w_in = gdn_w_in[j]
            n_extra = w_in.shape[1] - 4 * gdn_dim
            w_in_p = jnp.concatenate([w_in, jnp.zeros((d, LANES - n_extra), F32)], axis=1).astype(BF16)
            w_out = gdn_w_out[j].astype(BF16)
            proj = _mm(xp, w_in_p, gain=norm_mix[i], tn_cap=1408).reshape(bp, seq, -1)
            rows = _gdn_rows(proj[..., 4 * gdn_dim:], gdn_heads, GDN_CHUNK)
            o, sfin = _gdn_mixer(proj, rows, gdn_a_log[j], gdn_dt_bias[j], gdn_conv_w[j], gdn_out_norm[j],
                                 jnp.zeros((bp, CONV_W - 1, 3 * gdn_dim), F32),
                                 jnp.zeros((bp, gdn_heads, gdn_dk, gdn_dk), F32), None, gdn_tb, 1)
            xp = _mm(o.reshape(tp, gdn_dim), w_out, resid=xp)
            ssm_p.append(sfin)
            conv_p.append(proj[:, seq - (CONV_W - 1):, :3 * gdn_dim])
            proj_s = _mm(xs, w_in_p, gain=norm_mix[i], tn_cap=1408)[:bs]
            proj_c = jnp.pad(proj_s[:, None, :], ((0, 0), (0, GDN_CHUNK - 1), (0, 0)))
            rows_s = _gdn_rows(proj_c[..., 4 * gdn_dim:], gdn_heads, GDN_CHUNK)
            o_s, sfin_s = _gdn_mixer(proj_c, rows_s, gdn_a_log[j], gdn_dt_bias[j], gdn_conv_w[j], gdn_out_norm[j],
                                     state_conv[j], state_ssm[j], 1, GDN_CHUNK, gdn_heads)
            o_s = jnp.pad(o_s[:, 0, :], ((0, SAMPLE_PAD - bs), (0, 0)))
            xs = _mm(o_s, w_out, resid=xs)
            ssm_s.append(sfin_s)
            conv_s.append(jnp.concatenate([state_conv[j][:, 1:, :], proj_s[:, None, :3 * gdn_dim]], axis=1))
        else:
            w = _mla_weights(mla_w_in[j], mla_cq_norm[j], mla_ckv_norm[j], mla_w_uq[j], mla_q_norm[j],
                             mla_k_norm[j], mla_w_uk[j], mla_w_uv[j], mla_w_out[j])
            nh, kv_lora, vdim = w["nh"], w["kv_lora"], w["vdim"]
            kc = kv_lora + LANES
            proj = _mm(xp, w["w_in"], gain=norm_mix[i])
            lat, kr, ksc, qcat, kcat, kst = _mla_prep(proj, w, *tabs_p, tm=min(256, tp))
            kst_b = kst.reshape(SUBLANES, bp, seq).transpose(1, 0, 2)
            o = _mla_attention(qcat.reshape(nh, bp, seq, kc), kcat.reshape(bp, seq, kc), kst_b, w["wuv"],
                               nh, kv_lora, vdim, attn_tq)
            xp = _mm(o.reshape(tp, nh * vdim), w["w_out"], resid=xp)
            lat_p.append(lat.reshape(bp, seq, kv_lora))
            kr_p.append(kr.reshape(bp, seq, -1))
            ksc_p.append(ksc.reshape(bp, seq, nh))
            proj_s = _mm(xs, w["w_in"], gain=norm_mix[i])
            lat1, kr1, ksc1, qcat1, kcat1, kst1 = _mla_prep(proj_s, w, *tabs_s, tm=SAMPLE_PAD)
            qd = qcat1[:, :bs, :].transpose(1, 0, 2)
            ksself = kst1[:nh, :bs].T.reshape(bs, nh, 1)
            o_s = _mla_decode(page_table, qd, cache_latent, cache_krope, cache_kscale_t,
                              kcat1[:bs].reshape(bs, 1, kc), ksself, w["wuv"], nh, kv_lora, w["rope"], vdim,
                              float((w["nope"] + w["rope"]) ** -0.5), decode_pages, j)
            o_s = jnp.pad(o_s.reshape(bs, nh * vdim), ((0, SAMPLE_PAD - bs), (0, 0)))
            xs = _mm(o_s, w["w_out"], resid=xs)
            lat_s.append(lat1[:bs].reshape(bs, 1, kv_lora))
            kr_s.append(kr1[:bs].reshape(bs, 1, -1))
            ksc_s.append(ksc1[:bs].reshape(bs, 1, nh))
        wq = peer_w_q[i].astype(BF16)
        keys = peer_sub_keys[i].reshape(peer_heads * 2, n_keys, peer_half)
        u = peer_u[i].astype(BF16)
        vt = peer_v[i].astype(BF16).reshape(-1, PEER_SLICE, d).transpose(0, 2, 1)
        xp = _peer(xp, norm_ffn[i], wq, keys, u, vt, peer_heads, peer_tblk_p)
        xs = _peer(xs, norm_ffn[i], wq, keys, u, vt, peer_heads, SAMPLE_PAD)
    return (xp.reshape(bp, seq, d), xs[:bs].reshape(bs, 1, d),
            jnp.stack(lat_p), jnp.stack(kr_p), jnp.stack(ksc_p), jnp.stack(ssm_p), jnp.stack(conv_p),
            jnp.stack(lat_s), jnp.stack(kr_s), jnp.stack(ksc_s), jnp.stack(ssm_s), jnp.stack(conv_s))
```

```python
import functools
import math

import numpy as np
import jax
import jax.numpy as jnp
from jax import lax
from jax.experimental import pallas as pl
from jax.experimental.pallas import tpu as pltpu

F32 = jnp.float32
BF16 = jnp.bfloat16
HI = lax.Precision.HIGHEST

EPS = 1e-6
ROPE_THETA = 10000.0
GDN_CHUNK = 64
CONV_W = 4
PEER_TOPK = 16
LANES = 128
SUBLANES = 8
VMEM_LIMIT = 52 * 1024 * 1024
SAMPLE_PAD = 128
PEER_SLICE = 256

NT_DIMS = (((1,), (1,)), ((), ()))
TN_DIMS = (((0,), (0,)), ((), ()))


def _params(sem):
    return pltpu.CompilerParams(dimension_semantics=sem, vmem_limit_bytes=VMEM_LIMIT)


def _iota(shape, dim):
    return lax.broadcasted_iota(jnp.int32, shape, dim)


def _aligned(x, m):
    return x if isinstance(x, int) else pl.multiple_of(x, m)


def _silu(x):
    return x * jax.nn.sigmoid(x)


def _softplus(x):
    return jnp.maximum(x, 0.0) + jnp.log1p(jnp.exp(-jnp.abs(x)))


def _dot(a, b, precision=None):
    return jnp.dot(a, b, preferred_element_type=F32, precision=precision)


def _dot_nt(a, b, precision=None):
    return lax.dot_general(a, b, NT_DIMS, preferred_element_type=F32, precision=precision)


def _split(x, terms):
    out = []
    for _ in range(terms):
        t = x.astype(BF16)
        out.append(t)
        x = x - t.astype(F32)
    return out


def _dot3(a, b, dot=_dot):
    return dot(a[0], b[0]) + (dot(a[0], b[1]) + dot(a[1], b[0]))


def _dot_exact01(a3, b01, dot=_dot):
    return dot(a3[0], b01) + (dot(a3[1], b01) + dot(a3[2], b01))


def _mm_kernel(*refs, has_gain, has_resid, emit_xn):
    it = iter(refs)
    x_ref, w_ref = next(it), next(it)
    g_ref = next(it) if has_gain else None
    r_ref = next(it) if has_resid else None
    o_ref = next(it)
    xn_ref = next(it) if emit_xn else None
    xb_scr = next(it)

    @pl.when(pl.program_id(1) == 0)
    def _():
        x = x_ref[...].astype(F32)
        if has_gain:
            x = x * lax.rsqrt(jnp.mean(x * x, axis=-1, keepdims=True) + EPS) * g_ref[...]
        xb = x.astype(BF16)
        xb_scr[...] = xb
        if emit_xn:
            xn_ref[...] = xb

    y = _dot(xb_scr[...], w_ref[...])
    if has_resid:
        y = y + r_ref[...]
    o_ref[...] = y.astype(o_ref.dtype)


def _pick_tile(n, cap):
    if n <= cap:
        return n
    best = LANES
    for t in range(LANES, cap + 1, LANES):
        if n % t == 0:
            best = t
    return best


def _mm(x, w, gain=None, resid=None, emit_xn=False, tm_cap=1024, tn_cap=1024):
    m, k = x.shape
    n = w.shape[1]
    tm = _pick_tile(m, tm_cap)
    tn = _pick_tile(n, tn_cap)
    assert m % tm == 0 and n % tn == 0
    in_specs = [pl.BlockSpec((tm, k), lambda i, j: (i, 0)),
                pl.BlockSpec((k, tn), lambda i, j: (0, j))]
    args = [x, w]
    if gain is not None:
        in_specs.append(pl.BlockSpec((1, k), lambda i, j: (0, 0)))
        args.append(gain.reshape(1, k).astype(F32))
    if resid is not None:
        in_specs.append(pl.BlockSpec((tm, tn), lambda i, j: (i, j)))
        args.append(resid)
    out_shape = [jax.ShapeDtypeStruct((m, n), F32)]
    out_specs = [pl.BlockSpec((tm, tn), lambda i, j: (i, j))]
    if emit_xn:
        out_shape.append(jax.ShapeDtypeStruct((m, k), BF16))
        out_specs.append(pl.BlockSpec((tm, k), lambda i, j: (i, 0)))
    res = pl.pallas_call(
        functools.partial(_mm_kernel, has_gain=gain is not None, has_resid=resid is not None, emit_xn=emit_xn),
        grid=(m // tm, n // tn),
        in_specs=in_specs,
        out_specs=out_specs,
        out_shape=out_shape,
        scratch_shapes=[pltpu.VMEM((tm, k), BF16)],
        compiler_params=_params(("parallel", "arbitrary")),
        name="norm_matmul",
    )(*args)
    return res if emit_xn else res[0]


def _unit_lower_inverse(low, eye, ri, ci, chunk):
    base = min(16, chunk)
    assert chunk & (chunk - 1) == 0
    same = lambda size: (ri >> (size.bit_length() - 1)) == (ci >> (size.bit_length() - 1))
    pw = [-jnp.where(same(base), l, 0.0) for l in low]
    inv = [eye + p for p in pw]
    ps = [_split(p, 2) for p in pw]
    for _ in range(max(int(math.ceil(math.log2(base))) - 1, 0)):
        ps = [_split(_dot3(p, p), 2) for p in ps]
        inv = [t + _dot3(_split(t, 2), p) for t, p in zip(inv, ps)]
    size = base
    while size < chunk:
        off = jnp.logical_and(same(2 * size), jnp.logical_not(same(size)))
        ts = [_split(t, 2) for t in inv]
        x = [_dot3(t, _split(jnp.where(off, l, 0.0), 2)) for t, l in zip(ts, low)]
        x = [_dot3(_split(xi, 2), t) for xi, t in zip(x, ts)]
        inv = [t - xi for t, xi in zip(inv, x)]
        size *= 2
    return inv


def _gdn_kernel(q_ref, k_ref, v_ref, z_ref, rows_ref, alog_ref, dt_ref, cwq_ref, cwk_ref, cwv_ref,
                gout_ref, bq_ref, bk_ref, bv_ref, s0_ref, o_ref, sfin_ref,
                xq_scr, xk_scr, xv_scr, s_scr, *, tb, chunk, hb, valid_len):
    t = pl.program_id(2)
    dk = s_scr.shape[-1]
    hist = CONV_W - 1
    base = SUBLANES - hist
    ncb = tb // chunk

    @pl.when(t == 0)
    def _():
        s_scr[...] = s0_ref[0]
        xq_scr[base:SUBLANES, :] = bq_ref[0]
        xk_scr[base:SUBLANES, :] = bk_ref[0]
        xv_scr[base:SUBLANES, :] = bv_ref[0]

    def conv_silu(x_ref, w_ref, scr):
        scr[SUBLANES:SUBLANES + tb, :] = x_ref[0]
        y = w_ref[0:1, :] * scr[base:base + tb, :]
        for j in range(1, CONV_W):
            y = y + w_ref[j:j + 1, :] * scr[base + j:base + j + tb, :]
        scr[base:SUBLANES, :] = scr[tb + base:tb + SUBLANES, :]
        return _silu(y)

    def l2n(x):
        return x * lax.rsqrt(jnp.sum(x * x, axis=-1, keepdims=True) + EPS)

    q_all = conv_silu(q_ref, cwq_ref, xq_scr)
    k_all = conv_silu(k_ref, cwk_ref, xk_scr)
    v_all = conv_silu(v_ref, cwv_ref, xv_scr)

    ri = _iota((chunk, chunk), 0)
    ci = _iota((chunk, chunk), 1)
    eye = (ri == ci).astype(F32)
    incl = ri >= ci
    strict = ri > ci
    eye_b = eye.astype(BF16)
    tril_b = incl.astype(F32).astype(BF16)
    triu_b = (ri <= ci).astype(F32).astype(BF16)
    rid8 = _iota((SUBLANES, chunk), 0)

    units = [(hh, c) for c in range(ncb) for hh in range(hb)]
    qs, ks, vs, beta, g_col, decay, low = [], [], [], [], [], [], []
    for hh, c in units:
        hs = slice(hh * dk, (hh + 1) * dk)
        r0 = c * chunk
        qs.append(l2n(q_all[r0:r0 + chunk, hs]) * (dk ** -0.5))
        ks.append(l2n(k_all[r0:r0 + chunk, hs]))
        vs.append(v_all[r0:r0 + chunk, hs])
        raw = rows_ref[0, hh, c]
        m8 = jnp.where(rid8 == 0, jax.nn.sigmoid(raw),
                       jnp.where(rid8 == 1, -jnp.exp(alog_ref[hh]) * _softplus(raw + dt_ref[hh]), 0.0))
        if valid_len is not None:
            pos = t * tb + r0 + _iota((SUBLANES, chunk), 1)
            m8 = jnp.where(pos < valid_len, m8, 0.0)
        m3 = _split(m8, 3)
        flip = lambda a, b: _dot_nt(b, a)
        g_row = _dot_exact01(m3, triu_b)[1:2, :]
        beta.append(_dot_exact01(m3, eye_b, flip)[:, 0:1])
        g_col.append(_dot_exact01(m3, tril_b, flip)[:, 1:2])
        decay.append(jnp.exp(jnp.where(incl, g_col[-1] - g_row, -jnp.inf)))
    kb = [x.astype(BF16) for x in ks]
    for n in range(len(units)):
        low.append(jnp.where(strict, beta[n] * _dot_nt(kb[n], kb[n]) * decay[n], 0.0))
    inv = _unit_lower_inverse(low, eye, ri, ci, chunk)
    e_g = [jnp.exp(g) for g in g_col]
    w = [_dot3(_split(inv[n], 2),
               _split(jnp.concatenate([vs[n] * beta[n], ks[n] * (beta[n] * e_g[n])], axis=1), 2))
         for n in range(len(units))]
    qk = [_dot_nt(qs[n].astype(BF16), kb[n]) * decay[n] for n in range(len(units))]

    for n, (hh, c) in enumerate(units):
        hs = slice(hh * dk, (hh + 1) * dk)
        r0 = c * chunk
        g_last = g_col[n][chunk - 1:chunk, :]
        q_dec = qs[n] * e_g[n]
        k_dec = ks[n] * jnp.exp(g_last - g_col[n])
        s = s_scr[hh]
        sb = s.astype(BF16)
        u = w[n][:, :dk] - _dot(w[n][:, dk:].astype(BF16), sb)
        ub = u.astype(BF16)
        o = _dot(q_dec.astype(BF16), sb) + _dot(qk[n].astype(BF16), ub)
        s_scr[hh] = s * jnp.exp(g_last) + lax.dot_general(k_dec.astype(BF16), ub, TN_DIMS,
                                                           preferred_element_type=F32)
        on = o * lax.rsqrt(jnp.mean(o * o, axis=-1, keepdims=True) + EPS) * gout_ref[...]
        o_ref[0, r0:r0 + chunk, hs] = on * _silu(z_ref[0, r0:r0 + chunk, hs])

    @pl.when(t == pl.num_programs(2) - 1)
    def _():
        sfin_ref[0] = s_scr[...]


def _gdn_mixer(proj, rows, a_log, dt_bias, conv_w, g_out, buf0, s0, valid_len, tb, hb):
    b, t, _ = proj.shape
    nh = s0.shape[1]
    dk = s0.shape[2]
    chunk = rows.shape[-1]
    dim = nh * dk
    ng = nh // hb
    wdt = hb * dk
    col = lambda off: pl.BlockSpec((1, tb, wdt), lambda bi, h, ti, off=off: (bi, ti, off + h))
    cw = lambda off: pl.BlockSpec((CONV_W, wdt), lambda bi, h, ti, off=off: (0, off + h))
    bf = lambda off: pl.BlockSpec((1, CONV_W - 1, wdt), lambda bi, h, ti, off=off: (bi, 0, off + h))
    hscal = pl.BlockSpec((hb, 1, 1), lambda bi, h, ti: (h, 0, 0))
    o, sfin = pl.pallas_call(
        functools.partial(_gdn_kernel, tb=tb, chunk=chunk, hb=hb, valid_len=valid_len),
        grid=(b, ng, t // tb),
        in_specs=[col(0), col(ng), col(2 * ng), col(3 * ng),
                  pl.BlockSpec((1, hb, tb // chunk, SUBLANES, chunk), lambda bi, h, ti: (bi, h, ti, 0, 0)),
                  hscal, hscal, cw(0), cw(ng), cw(2 * ng),
                  pl.BlockSpec((1, dk), lambda bi, h, ti: (0, 0)),
                  bf(0), bf(ng), bf(2 * ng),
                  pl.BlockSpec((1, hb, dk, dk), lambda bi, h, ti: (bi, h, 0, 0))],
        out_specs=[pl.BlockSpec((1, tb, wdt), lambda bi, h, ti: (bi, ti, h)),
                   pl.BlockSpec((1, hb, dk, dk), lambda bi, h, ti: (bi, h, 0, 0))],
        out_shape=[jax.ShapeDtypeStruct((b, t, dim), F32),
                   jax.ShapeDtypeStruct((b, nh, dk, dk), F32)],
        scratch_shapes=[pltpu.VMEM((tb + SUBLANES, wdt), F32)] * 3 + [pltpu.VMEM((hb, dk, dk), F32)],
        compiler_params=_params(("parallel", "parallel", "arbitrary")),
        name="gdn_chunked",
    )(proj, proj, proj, proj, rows, a_log.reshape(nh, 1, 1), dt_bias.reshape(nh, 1, 1),
      conv_w, conv_w, conv_w, g_out.reshape(1, dk), buf0, buf0, buf0, s0)
    return o, sfin


def _gdn_rows(ba, nh, chunk):
    b, t, _ = ba.shape
    x = jnp.stack([ba[..., :nh], ba[..., nh:2 * nh]], axis=-1)
    x = x.reshape(b, t // chunk, chunk, nh, 2).transpose(0, 3, 1, 4, 2)
    return jnp.pad(x, ((0, 0),) * 3 + ((0, SUBLANES - 2), (0, 0)))


def _rope_block(x, cos_t, sin_a, sin_b):
    q4 = x.shape[-1] // 4
    return x * cos_t + pltpu.roll(x, 3 * q4, 1) * sin_a + pltpu.roll(x, q4, 1) * sin_b


def _mla_prep_kernel(p_ref, gcq_ref, gckv_ref, wuq_ref, gqn_ref, gqr_ref, wuk_ref, seg_ref, segt_ref,
                     seg8_ref, cos_ref, sina_ref, sinb_ref,
                     lat_ref, kr_ref, ksc_ref, qcat_ref, kcat_ref, kst_ref,
                     *, q_lora, kv_lora, nh, nope, rope_dim, scale):
    qk_dim = nope + rope_dim
    hn = nh * nope
    x = p_ref[...]
    cq = x[:, :q_lora]
    cq = cq * lax.rsqrt(jnp.mean(cq * cq, axis=-1, keepdims=True) + EPS) * gcq_ref[...]
    ckv = x[:, q_lora:q_lora + kv_lora]
    ckv = ckv * lax.rsqrt(jnp.mean(ckv * ckv, axis=-1, keepdims=True) + EPS) * gckv_ref[...]
    cos_t, sin_a, sin_b = cos_ref[...], sina_ref[...], sinb_ref[...]
    krb = _rope_block(x[:, q_lora + kv_lora:], cos_t, sin_a, sin_b)
    lat_ref[...] = ckv
    kr_ref[...] = krb[:, :rope_dim]
    ckv_b = ckv.astype(BF16)
    kcat_ref[:, :kv_lora] = ckv_b
    kcat_ref[:, kv_lora:] = krb.astype(BF16)

    qf = _dot(cq.astype(BF16), wuq_ref[...])
    qn = qf[:, :hn]
    qr = jnp.concatenate([_rope_block(qf[:, hn + h * nope:hn + (h + 1) * nope], cos_t, sin_a, sin_b)
                          for h in range(nh)], axis=1)
    ssq = _dot_exact01(_split(qn * qn + qr * qr, 3), seg_ref[...])
    inv = lax.rsqrt(ssq * (1.0 / qk_dim) + EPS)
    inv_x = _dot_exact01(_split(inv, 3), segt_ref[...])
    qn = qn * inv_x * gqn_ref[...]
    qr = qr * inv_x * gqr_ref[...]

    k_nope = _dot(ckv_b, wuk_ref[...])
    k2 = k_nope * k_nope
    kr2 = krb * krb
    k2s = _split(k2, 3)
    ksq = _dot_exact01(k2s, seg_ref[...]) + jnp.sum(kr2, axis=-1, keepdims=True)
    ksc = lax.rsqrt(ksq * (1.0 / qk_dim) + EPS)
    ksc_ref[...] = ksc[:, :nh]
    flip = lambda a, b: _dot_nt(b, a)
    ksq_t = (_dot_exact01(k2s, seg8_ref[...], flip)
             + _dot_exact01(_split(kr2, 3), jnp.ones((SUBLANES, kr2.shape[1]), BF16), flip))
    kst_ref[...] = lax.rsqrt(ksq_t * (1.0 / qk_dim) + EPS) * scale

    for h in range(nh):
        sl = slice(h * nope, (h + 1) * nope)
        q_lat = _dot_nt(qn[:, sl].astype(BF16), wuk_ref[:, sl])
        qcat_ref[h, :, :kv_lora] = q_lat.astype(BF16)
        qcat_ref[h, :, kv_lora:] = qr[:, sl].astype(BF16)


def _mla_prep(proj, w, cos_t, sin_a, sin_b, tm):
    m = proj.shape[0]
    nh, nope, rope_dim, q_lora, kv_lora = w["nh"], w["nope"], w["rope"], w["q_lora"], w["kv_lora"]
    kc = kv_lora + LANES
    full = lambda a: pl.BlockSpec(a.shape, lambda i: (0,) * a.ndim)
    rows = lambda n: pl.BlockSpec((tm, n), lambda i: (i, 0))
    consts = [w["gcq"], w["gckv"], w["wuq"], w["gqn"], w["gqr"], w["wuk"], w["seg"], w["segt"], w["seg8"]]
    return pl.pallas_call(
        functools.partial(_mla_prep_kernel, q_lora=q_lora, kv_lora=kv_lora, nh=nh, nope=nope,
                          rope_dim=rope_dim, scale=float((nope + rope_dim) ** -0.5)),
        grid=(m // tm,),
        in_specs=[rows(proj.shape[1])] + [full(a) for a in consts] + [rows(LANES)] * 3,
        out_specs=[rows(kv_lora), rows(rope_dim), rows(nh),
                   pl.BlockSpec((nh, tm, kc), lambda i: (0, i, 0)), rows(kc),
                   pl.BlockSpec((SUBLANES, tm), lambda i: (0, i))],
        out_shape=[jax.ShapeDtypeStruct((m, kv_lora), F32), jax.ShapeDtypeStruct((m, rope_dim), F32),
                   jax.ShapeDtypeStruct((m, nh), F32), jax.ShapeDtypeStruct((nh, m, kc), BF16),
                   jax.ShapeDtypeStruct((m, kc), BF16), jax.ShapeDtypeStruct((SUBLANES, m), F32)],
        compiler_params=_params(("parallel",)),
        name="mla_prep",
    )(proj, *consts, cos_t, sin_a, sin_b)


def _attn_kernel(q_ref, k_ref, ks_ref, wuv_ref, o_ref, m_scr, l_scr, acc_scr, *, nh, kv_lora, vdim, tq, tk):
    qi, kj = pl.program_id(1), pl.program_id(2)

    @pl.when(kj == 0)
    def _():
        m_scr[...] = jnp.full(m_scr.shape, -jnp.inf, F32)
        l_scr[...] = jnp.zeros(l_scr.shape, F32)
        acc_scr[...] = jnp.zeros(acc_scr.shape, F32)

    def update(diagonal):
        kblk = k_ref[0]
        vblk = kblk[:, :kv_lora]
        if diagonal:
            causal = _iota((tq, tk), 1) <= _iota((tq, tk), 0)
        for h in range(nh):
            s = _dot_nt(q_ref[h, 0], kblk) * ks_ref[0, h:h + 1, :]
            if diagonal:
                s = jnp.where(causal, s, -jnp.inf)
            m_old = m_scr[h]
            m_new = jnp.maximum(m_old, jnp.max(s, axis=-1, keepdims=True))
            alpha = jnp.exp(m_old - m_new)
            p = jnp.exp(s - m_new)
            l_scr[h] = alpha * l_scr[h] + jnp.sum(p, axis=-1, keepdims=True)
            acc_scr[h] = alpha * acc_scr[h] + _dot(p.astype(BF16), vblk)
            m_scr[h] = m_new

    @pl.when(kj < qi)
    def _():
        update(False)

    @pl.when(kj == qi)
    def _():
        update(True)
        for h in range(nh):
            o_lat = acc_scr[h] / l_scr[h]
            o_ref[0, :, h * vdim:(h + 1) * vdim] = _dot(o_lat.astype(BF16), wuv_ref[:, h * vdim:(h + 1) * vdim])


def _mla_attention(qcat, kcat, kst, wuv, nh, kv_lora, vdim, tq):
    _, b, t, kc = qcat.shape
    tk = tq
    return pl.pallas_call(
        functools.partial(_attn_kernel, nh=nh, kv_lora=kv_lora, vdim=vdim, tq=tq, tk=tk),
        grid=(b, t // tq, t // tk),
        in_specs=[pl.BlockSpec((nh, 1, tq, kc), lambda bi, qi, kj: (0, bi, qi, 0)),
                  pl.BlockSpec((1, tk, kc), lambda bi, qi, kj: (bi, jnp.minimum(kj, qi), 0)),
                  pl.BlockSpec((1, SUBLANES, tk), lambda bi, qi, kj: (bi, 0, jnp.minimum(kj, qi))),
                  pl.BlockSpec(wuv.shape, lambda bi, qi, kj: (0, 0))],
        out_specs=pl.BlockSpec((1, tq, nh * vdim), lambda bi, qi, kj: (bi, qi, 0)),
        out_shape=jax.ShapeDtypeStruct((b, t, nh * vdim), F32),
        scratch_shapes=[pltpu.VMEM((nh, tq, 1), F32), pltpu.VMEM((nh, tq, 1), F32),
                        pltpu.VMEM((nh, tq, kv_lora), F32)],
        compiler_params=_params(("parallel", "parallel", "arbitrary")),
        name="mla_prefill_attention",
    )(qcat, kcat, kst, wuv)


def _decode_kernel(pt_ref, q_ref, *refs, nh, kv_lora, rope_dim, vdim, scale, npg):
    lat_refs, kr_refs, ks_refs = refs[:npg], refs[npg:2 * npg], refs[2 * npg:3 * npg]
    kself_ref, ksself_ref, wuv_ref, o_ref, m_scr, l_scr, acc_scr = refs[3 * npg:]
    p = pl.program_id(1)

    @pl.when(p == 0)
    def _():
        m_scr[...] = jnp.full(m_scr.shape, -jnp.inf, F32)
        l_scr[...] = jnp.zeros(l_scr.shape, F32)
        acc_scr[...] = jnp.zeros(acc_scr.shape, F32)

    q = q_ref[0]
    q_lat, q_rope = q[:, :kv_lora], q[:, kv_lora:kv_lora + rope_dim]
    lat = [r[0, 0].astype(BF16) for r in lat_refs]
    s = jnp.concatenate(
        [(_dot_nt(q_lat, lat[g]) + _dot_nt(q_rope, kr_refs[g][0, 0].astype(BF16))) * (ks_refs[g][0] * scale)
         for g in range(npg)], axis=1)
    m_old = m_scr[...]
    m_new = jnp.maximum(m_old, jnp.max(s, axis=-1, keepdims=True))
    alpha = jnp.exp(m_old - m_new)
    pr = jnp.exp(s - m_new)
    l_scr[...] = alpha * l_scr[...] + jnp.sum(pr, axis=-1, keepdims=True)
    page = lat[0].shape[0]
    pv = _dot(pr[:, :page].astype(BF16), lat[0])
    for g in range(1, npg):
        pv = pv + _dot(pr[:, g * page:(g + 1) * page].astype(BF16), lat[g])
    acc_scr[...] = alpha * acc_scr[...] + pv
    m_scr[...] = m_new

    @pl.when(p == pl.num_programs(1) - 1)
    def _():
        kself = kself_ref[0].astype(F32)
        s_self = jnp.sum(q.astype(F32) * kself, axis=-1, keepdims=True) * ksself_ref[0]
        m_old = m_scr[...]
        m_new = jnp.maximum(m_old, s_self)
        alpha = jnp.exp(m_old - m_new)
        p_self = jnp.exp(s_self - m_new)
        l_fin = alpha * l_scr[...] + p_self
        acc = alpha * acc_scr[...] + p_self.astype(BF16).astype(F32) * kself[:, :kv_lora]
        o_all = _dot((acc / l_fin).astype(BF16), wuv_ref[...])
        for h in range(nh):
            o_ref[0, :, h * vdim:(h + 1) * vdim] = o_all[h:h + 1, h * vdim:(h + 1) * vdim]


def _mla_decode(page_table, qd, cache_lat, cache_kr, cache_kst, kself, ksself, wuv, nh, kv_lora, rope_dim, vdim,
                scale, npg, layer):
    b, n_pages = page_table.shape
    page = cache_lat.shape[2]
    kc = qd.shape[-1]
    assert n_pages % npg == 0
    pg = lambda shape, g: pl.BlockSpec(shape, lambda bi, p, pt, g=g: (layer, pt[bi, p * npg + g], 0, 0))
    grid_spec = pltpu.PrefetchScalarGridSpec(
        num_scalar_prefetch=1,
        grid=(b, n_pages // npg),
        in_specs=([pl.BlockSpec((1, nh, kc), lambda bi, p, pt: (bi, 0, 0))]
                  + [pg((1, 1, page, kv_lora), g) for g in range(npg)]
                  + [pg((1, 1, page, rope_dim), g) for g in range(npg)]
                  + [pl.BlockSpec((1, nh, page), lambda bi, p, pt, g=g: (pt[bi, p * npg + g], 0, 0))
                     for g in range(npg)]
                  + [pl.BlockSpec((1, 1, kc), lambda bi, p, pt: (bi, 0, 0)),
                     pl.BlockSpec((1, nh, 1), lambda bi, p, pt: (bi, 0, 0)),
                     pl.BlockSpec(wuv.shape, lambda bi, p, pt: (0, 0))]),
        out_specs=pl.BlockSpec((1, 1, nh * vdim), lambda bi, p, pt: (bi, 0, 0)),
        scratch_shapes=[pltpu.VMEM((nh, 1), F32), pltpu.VMEM((nh, 1), F32), pltpu.VMEM((nh, kv_lora), F32)],
    )
    return pl.pallas_call(
        functools.partial(_decode_kernel, nh=nh, kv_lora=kv_lora, rope_dim=rope_dim, vdim=vdim, scale=scale,
                          npg=npg),
        grid_spec=grid_spec,
        out_shape=jax.ShapeDtypeStruct((b, 1, nh * vdim), F32),
        compiler_params=_params(("parallel", "arbitrary")),
        name="mla_paged_decode",
    )(page_table, qd, *([cache_lat] * npg), *([cache_kr] * npg), *([cache_kst] * npg), kself, ksself, wuv)


def _oddeven_merge_sort_pairs(n):
    pairs = []

    def merge(lo, m, r):
        step = r * 2
        if step < m:
            merge(lo, m, step)
            merge(lo + r, m, step)
            pairs.extend((i, i + r) for i in range(lo + r, lo + m - r, step))
        else:
            pairs.append((lo, lo + r))

    def sort(lo, m):
        if m > 1:
            sort(lo, m // 2)
            sort(lo + m // 2, m // 2)
            merge(lo, m, 1)

    sort(0, n)
    return pairs


def _cmp_exchange(v, i, j):
    a, b = v[i], v[j]
    if b is None:
        return
    if a is None:
        v[i], v[j] = b, None
        return
    v[i], v[j] = jnp.maximum(a, b), jnp.minimum(a, b)


def _top_sorted(v):
    k = len(v)
    v = list(v)
    for i, j in _oddeven_merge_sort_pairs(k):
        _cmp_exchange(v, i, j)
    shift = SUBLANES // 2
    while shift >= 1:
        rot = [None if x is None else pltpu.roll(x, shift, 0) for x in v]
        v = [rot[k - 1 - i] if v[i] is None else (v[i] if rot[k - 1 - i] is None
                                                   else jnp.maximum(v[i], rot[k - 1 - i])) for i in range(k)]
        d = k // 2
        while d >= 1:
            for i in range(k):
                if i & d == 0:
                    _cmp_exchange(v, i, i + d)
            d //= 2
        shift //= 2
    return v


def _peer_select_kernel(q_ref, khi_ref, klo_ref, s0_ref, s1_ref, e0_ref, e1_ref, thr_ref, *, nh, half, topk):
    tb = q_ref.shape[0]
    n_keys = khi_ref.shape[1]
    sub = _iota((SUBLANES, tb), 0)
    hid = _iota((nh, tb), 0)
    neg = jnp.full((SUBLANES, tb), -jnp.inf, F32)

    def spread(vals):
        out = vals[SUBLANES - 1]
        for r in range(SUBLANES - 2, -1, -1):
            out = jnp.where(sub == r, vals[r], out)
        return out

    thr_all = jnp.zeros((nh, tb), F32)
    for h in range(nh):
        s, sv = [], []
        for p in range(2):
            hp = 2 * h + p
            sp = _dot3((khi_ref[hp], klo_ref[hp]), _split(q_ref[:, hp * half:(hp + 1) * half], 2),
                       dot=_dot_nt)
            s.append(sp)
            tiles = [sp[r * SUBLANES:(r + 1) * SUBLANES, :] for r in range(n_keys // SUBLANES)]
            sv.append(_top_sorted(tiles)[:topk])
        a, b = sv
        b_lo, b_hi = spread(b[:SUBLANES]), spread(b[SUBLANES:])
        a_lo = jnp.where(sub >= 4, spread(a[:SUBLANES]), neg)
        a_hi = spread(a[SUBLANES:])
        cand = [a[0] + b_lo, a[0] + b_hi, a[1] + b_lo, a[2] + b_lo, a[3] + b_lo,
                a_lo + b[0], a_hi + b[0], a_lo + b[1], a_lo + b[2]]
        ts = _top_sorted(cand + [None] * (topk - len(cand)))
        mx = ts[0][0:1, :]
        tot = jnp.exp(ts[0][0:1, :] - mx)
        for r in range(1, topk):
            tot = tot + jnp.exp(ts[r][0:1, :] - mx)
        lse = mx + jnp.log(tot)
        m0 = a[0][0:1, :]
        s0_ref[h] = s[0]
        s1_ref[h] = s[1]
        e0_ref[h] = jnp.exp(s[0] - m0)
        e1_ref[h] = jnp.exp(s[1] - (lse - m0))
        thr_all = jnp.where(hid == h, ts[topk - 1][0:1, :], thr_all)
    thr_ref[...] = thr_all


def _peer_select(q, keys, nh, tb):
    t = q.shape[0]
    n_keys, half = keys[0].shape[1], keys[0].shape[2]
    assert PEER_TOPK == 16 and n_keys == 16 * SUBLANES
    big = pl.BlockSpec((nh, n_keys, tb), lambda i: (0, 0, i))
    keys_hi, keys_lo = keys
    return pl.pallas_call(
        functools.partial(_peer_select_kernel, nh=nh, half=half, topk=PEER_TOPK),
        grid=(t // tb,),
        in_specs=[pl.BlockSpec((tb, q.shape[1]), lambda i: (i, 0)),
                  pl.BlockSpec(keys_hi.shape, lambda i: (0, 0, 0)),
                  pl.BlockSpec(keys_lo.shape, lambda i: (0, 0, 0))],
        out_specs=[big, big, big, big, pl.BlockSpec((nh, tb), lambda i: (0, i))],
        out_shape=[jax.ShapeDtypeStruct((nh, n_keys, t), F32)] * 4 + [jax.ShapeDtypeStruct((nh, t), F32)],
        compiler_params=_params(("parallel",)),
        name="peer_select",
    )(q, keys_hi, keys_lo)


def _peer_dense_kernel(xn_ref, u_ref, unext_ref, vt_ref, vtprev_ref, s0_ref, s1_ref, e0_ref, e1_ref, thr_ref,
                       res_ref, o_ref, ht_scr, ht2_scr, a_scr, a2_scr, acc_scr, s0x_scr, e0x_scr,
                       *, nh, n_keys, e_blk, t_blk, rows_per_slice):
    e = pl.program_id(1)
    n_i = e_blk // n_keys
    n_sl = n_i // rows_per_slice
    e_sl = rows_per_slice * n_keys
    sqrt_half = float(np.sqrt(0.5))
    ht_bufs, a_bufs = (ht_scr, ht2_scr), (a_scr, a2_scr)
    last = (n_sl - 1) % 2

    @pl.when(e == 0)
    def _():
        acc_scr[...] = jnp.zeros(acc_scr.shape, F32)
        a_bufs[last][...] = jnp.zeros(a_bufs[last].shape, BF16)
        ht_bufs[0][...] = _dot_nt(xn_ref[...], u_ref[0:e_sl, :]).T

    i0 = pl.multiple_of(e * n_i, n_i)
    for h in range(nh):
        srows = s0_ref[h, pl.ds(i0, n_i), :]
        erows = e0_ref[h, pl.ds(i0, n_i), :]
        for ii in range(n_i):
            s0x_scr[ii, h:h + 1, :] = srows[ii:ii + 1, :]
            e0x_scr[ii, h:h + 1, :] = erows[ii:ii + 1, :]

    def pre_activations(m):
        u_sl = unext_ref[...] if m == n_sl else u_ref[m * e_sl:(m + 1) * e_sl, :]
        ht_bufs[m % 2][...] = _dot_nt(xn_ref[...], u_sl).T

    def gated_activations(m):
        nv = n_keys // SUBLANES
        for r in range(rows_per_slice):
            ii = m * rows_per_slice + r
            for tc in range(t_blk // LANES):
                ls = slice(tc * LANES, (tc + 1) * LANES)
                gate = [None] * nv
                for h in range(nh):
                    s0r = jnp.broadcast_to(s0x_scr[ii, h:h + 1, ls], (SUBLANES, LANES))
                    e0r = jnp.broadcast_to(e0x_scr[ii, h:h + 1, ls], (SUBLANES, LANES))
                    thr = jnp.broadcast_to(thr_ref[h:h + 1, ls], (SUBLANES, LANES))
                    for v in range(nv):
                        ks = slice(v * SUBLANES, (v + 1) * SUBLANES)
                        ssum = s1_ref[h, ks, ls] + s0r
                        term = jnp.where(ssum >= thr, e1_ref[h, ks, ls], 0.0) * e0r
                        gate[v] = term if gate[v] is None else gate[v] + term
                for v in range(0, nv, 2):
                    rs = slice(r * n_keys + v * SUBLANES, r * n_keys + (v + 2) * SUBLANES)
                    hv = ht_bufs[m % 2][rs, ls]
                    act = 0.5 * hv * (1.0 + lax.erf(hv * sqrt_half))
                    a_bufs[m % 2][rs, ls] = (act * jnp.concatenate(gate[v:v + 2], axis=0)).astype(BF16)

    def accumulate(m):
        v_sl = vtprev_ref[0] if m == -1 else vt_ref[m]
        acc_scr[...] += _dot(v_sl, a_bufs[m % 2][...])

    assert n_sl % 2 == 0
    for m in range(n_sl):
        accumulate(m - 1)
        pre_activations(m + 1)
        gated_activations(m)

    @pl.when(e == pl.num_programs(1) - 1)
    def _():
        accumulate(n_sl - 1)
        o_ref[...] = res_ref[...] + acc_scr[...].T


def _peer_dense(xn, u, vt, s0t, s1t, e0t, e1t, thr, resid, t_blk, e_blk):
    t, d = xn.shape
    n_exp = u.shape[0]
    nh, n_keys, _ = s0t.shape
    assert e_blk == SUBLANES * n_keys and nh <= SUBLANES
    e_sl = vt.shape[2]
    n_sl = e_blk // e_sl
    ne = n_exp // e_blk
    big = pl.BlockSpec((nh, n_keys, t_blk), lambda ti, e: (0, 0, ti))
    return pl.pallas_call(
        functools.partial(_peer_dense_kernel, nh=nh, n_keys=n_keys, e_blk=e_blk, t_blk=t_blk,
                          rows_per_slice=e_sl // n_keys),
        grid=(t // t_blk, ne),
        in_specs=[pl.BlockSpec((t_blk, d), lambda ti, e: (ti, 0)),
                  pl.BlockSpec((e_blk, d), lambda ti, e: (e, 0)),
                  pl.BlockSpec((e_sl, d), lambda ti, e: (jnp.minimum(e + 1, ne - 1) * n_sl, 0)),
                  pl.BlockSpec((n_sl, d, e_sl), lambda ti, e: (e, 0, 0)),
                  pl.BlockSpec((1, d, e_sl), lambda ti, e: (jnp.maximum(e * n_sl - 1, 0), 0, 0)),
                  big, big, big, big,
                  pl.BlockSpec((nh, t_blk), lambda ti, e: (0, ti)),
                  pl.BlockSpec((t_blk, d), lambda ti, e: (ti, 0))],
        out_specs=pl.BlockSpec((t_blk, d), lambda ti, e: (ti, 0)),
        out_shape=jax.ShapeDtypeStruct((t, d), F32),
        scratch_shapes=[pltpu.VMEM((e_sl, t_blk), F32), pltpu.VMEM((e_sl, t_blk), F32),
                        pltpu.VMEM((e_sl, t_blk), BF16), pltpu.VMEM((e_sl, t_blk), BF16),
                        pltpu.VMEM((d, t_blk), F32),
                        pltpu.VMEM((e_blk // n_keys, SUBLANES, t_blk), F32),
                        pltpu.VMEM((e_blk // n_keys, SUBLANES, t_blk), F32)],
        compiler_params=_params(("parallel", "arbitrary")),
        name="peer_dense",
    )(xn, u, u, vt, vt, s0t, s1t, e0t, e1t, thr, resid)


def _peer(x, gain, wq, keys, u, vt, nh, t_blk):
    q, xn = _mm(x, wq, gain=gain, emit_xn=True)
    s0t, s1t, e0t, e1t, thr = _peer_select(q, keys, nh, LANES)
    return _peer_dense(xn, u, vt, s0t, s1t, e0t, e1t, thr, x, t_blk, SUBLANES * keys[0].shape[1])


def _rope_tables(pos, rope_dim):
    half = rope_dim // 2
    freq = ROPE_THETA ** (-jnp.arange(half, dtype=F32) / half)
    ang = pos.astype(F32)[:, None] * freq[None, :]
    cos, sin = jnp.cos(ang), jnp.sin(ang)
    z = jnp.zeros_like(cos)
    pad = jnp.zeros((pos.shape[0], LANES - 4 * half), F32)
    cat = lambda *xs: jnp.concatenate(xs + (pad,), axis=1) if pad.shape[1] else jnp.concatenate(xs, axis=1)
    return cat(cos, cos, z, z), cat(-sin, z, z, z), cat(z, sin, z, z)


def _mla_weights(w_in, g_cq, g_ckv, w_uq, g_qn, g_kn, w_uk, w_uv, w_out):
    d = w_in.shape[0]
    q_lora, kv_lora = g_cq.shape[0], g_ckv.shape[0]
    rope_dim = w_in.shape[1] - q_lora - kv_lora
    _, nh, nope = w_uk.shape
    vdim = w_uv.shape[2]
    qk_dim = nope + rope_dim
    assert nope == LANES and 2 * rope_dim <= LANES and rope_dim % 2 == 0 and nh <= SUBLANES
    w_in_p = jnp.concatenate([w_in, jnp.zeros((d, LANES - rope_dim), F32)], axis=1).astype(BF16)
    wq3 = w_uq.reshape(q_lora, nh, qk_dim)
    w_nope = wq3[:, :, :nope].reshape(q_lora, nh * nope)
    w_rope = jnp.pad(wq3[:, :, nope:], ((0, 0), (0, 0), (0, nope - rope_dim))).reshape(q_lora, nh * nope)
    gg = g_qn * g_kn
    hn = nh * nope
    head_of = np.arange(hn) // nope
    seg = np.zeros((hn, LANES), np.float32)
    seg[np.arange(hn), head_of] = 1.0
    seg8 = np.zeros((SUBLANES, hn), np.float32)
    seg8[head_of, np.arange(hn)] = 1.0
    return dict(
        nh=nh, nope=nope, rope=rope_dim, q_lora=q_lora, kv_lora=kv_lora, vdim=vdim,
        w_in=w_in_p,
        gcq=g_cq.reshape(1, -1), gckv=g_ckv.reshape(1, -1),
        wuq=jnp.concatenate([w_nope, w_rope], axis=1).astype(BF16),
        gqn=jnp.tile(gg[:nope], nh).reshape(1, hn),
        gqr=jnp.tile(jnp.pad(gg[nope:], (0, nope - rope_dim)), nh).reshape(1, hn),
        wuk=w_uk.reshape(kv_lora, hn).astype(BF16),
        wuv=w_uv.reshape(kv_lora, nh * vdim).astype(BF16),
        w_out=w_out.astype(BF16),
        seg=jnp.asarray(seg, BF16), segt=jnp.asarray(seg.T.copy(), BF16), seg8=jnp.asarray(seg8, BF16),
    )


def kernel(x_prompt, x_sample, cache_latent, cache_krope, cache_kscale, page_table, state_ssm, state_conv,
           norm_mix, norm_ffn, gdn_w_in, gdn_conv_w, gdn_a_log, gdn_dt_bias, gdn_out_norm, gdn_w_out,
           mla_w_in, mla_cq_norm, mla_ckv_norm, mla_w_uq, mla_q_norm, mla_k_norm, mla_w_uk, mla_w_uv, mla_w_out,
           peer_w_q, peer_sub_keys, peer_u, peer_v):
    bp, seq, d = x_prompt.shape
    bs, dec_seq, _ = x_sample.shape
    assert dec_seq == 1 and bs <= SAMPLE_PAD
    depth = norm_mix.shape[0]
    n_pages, page = page_table.shape[1], cache_latent.shape[2]
    past_len = n_pages * page
    gdn_heads, gdn_dk = state_ssm.shape[2], state_ssm.shape[3]
    gdn_dim = gdn_heads * gdn_dk
    peer_heads, n_keys, peer_half = peer_sub_keys.shape[1], peer_sub_keys.shape[3], peer_sub_keys.shape[4]
    assert n_keys == LANES and peer_half == LANES and gdn_dk == LANES

    tp = bp * seq
    xp = x_prompt.reshape(tp, d)
    xs = jnp.pad(x_sample.reshape(bs, d), ((0, SAMPLE_PAD - bs), (0, 0)))
    gdn_tb = min(512, seq)
    attn_tq = min(512, seq)
    peer_tblk_p = min(512, tp)
    decode_pages = math.gcd(n_pages, 8)

    rope_dim = mla_w_in.shape[2] - mla_cq_norm.shape[1] - mla_ckv_norm.shape[1]
    tabs_p = _rope_tables(jnp.tile(jnp.arange(seq), bp), rope_dim)
    tabs_s = _rope_tables(jnp.full((SAMPLE_PAD,), past_len), rope_dim)

    lat_p, kr_p, ksc_p, ssm_p, conv_p = [], [], [], [], []
    lat_s, kr_s, ksc_s, ssm_s, conv_s = [], [], [], [], []
    for i in range(depth):
        j = i // 2
        if i % 2 == 0:
            w_in = gdn_w_in[j]
            n_extra = w_in.shape[1] - 4 * gdn_dim
            w_in_p = jnp.concatenate([w_in, jnp.zeros((d, LANES - n_extra), F32)], axis=1).astype(BF16)
            w_out = gdn_w_out[j].astype(BF16)
            proj = _mm(xp, w_in_p, gain=norm_mix[i], tn_cap=1408).reshape(bp, seq, -1)
            rows = _gdn_rows(proj[..., 4 * gdn_dim:], gdn_heads, GDN_CHUNK)
            o, sfin = _gdn_mixer(proj, rows, gdn_a_log[j], gdn_dt_bias[j], gdn_conv_w[j], gdn_out_norm[j],
                                 jnp.zeros((bp, CONV_W - 1, 3 * gdn_dim), F32),
                                 jnp.zeros((bp, gdn_heads, gdn_dk, gdn_dk), F32), None, gdn_tb,
                                 math.gcd(gdn_heads, 2))
            xp = _mm(o.reshape(tp, gdn_dim), w_out, resid=xp)
            ssm_p.append(sfin)
            conv_p.append(proj[:, seq - (CONV_W - 1):, :3 * gdn_dim])
            proj_s = _mm(xs, w_in_p, gain=norm_mix[i], tn_cap=1408)[:bs]
            proj_c = jnp.pad(proj_s[:, None, :], ((0, 0), (0, GDN_CHUNK - 1), (0, 0)))
            rows_s = _gdn_rows(proj_c[..., 4 * gdn_dim:], gdn_heads, GDN_CHUNK)
            o_s, sfin_s = _gdn_mixer(proj_c, rows_s, gdn_a_log[j], gdn_dt_bias[j], gdn_conv_w[j], gdn_out_norm[j],
                                     state_conv[j], state_ssm[j], 1, GDN_CHUNK, gdn_heads)
            o_s = jnp.pad(o_s[:, 0, :], ((0, SAMPLE_PAD - bs), (0, 0)))
            xs = _mm(o_s, w_out, resid=xs)
            ssm_s.append(sfin_s)
            conv_s.append(jnp.concatenate([state_conv[j][:, 1:, :], proj_s[:, None, :3 * gdn_dim]], axis=1))
        else:
            w = _mla_weights(mla_w_in[j], mla_cq_norm[j], mla_ckv_norm[j], mla_w_uq[j], mla_q_norm[j],
                             mla_k_norm[j], mla_w_uk[j], mla_w_uv[j], mla_w_out[j])
            nh, kv_lora, vdim = w["nh"], w["kv_lora"], w["vdim"]
            kc = kv_lora + LANES
            proj = _mm(xp, w["w_in"], gain=norm_mix[i])
            lat, kr, ksc, qcat, kcat, kst = _mla_prep(proj, w, *tabs_p, tm=min(256, tp))
            kst_b = kst.reshape(SUBLANES, bp, seq).transpose(1, 0, 2)
            o = _mla_attention(qcat.reshape(nh, bp, seq, kc), kcat.reshape(bp, seq, kc), kst_b, w["wuv"],
                               nh, kv_lora, vdim, attn_tq)
            xp = _mm(o.reshape(tp, nh * vdim), w["w_out"], resid=xp)
            lat_p.append(lat.reshape(bp, seq, kv_lora))
            kr_p.append(kr.reshape(bp, seq, -1))
            ksc_p.append(ksc.reshape(bp, seq, nh))
            proj_s = _mm(xs, w["w_in"], gain=norm_mix[i])
            lat1, kr1, ksc1, qcat1, kcat1, kst1 = _mla_prep(proj_s, w, *tabs_s, tm=SAMPLE_PAD)
            qd = qcat1[:, :bs, :].transpose(1, 0, 2)
            ksself = kst1[:nh, :bs].T.reshape(bs, nh, 1)
            o_s = _mla_decode(page_table, qd, cache_latent, cache_krope, cache_kscale[j].transpose(0, 2, 1),
                              kcat1[:bs].reshape(bs, 1, kc), ksself, w["wuv"], nh, kv_lora, w["rope"], vdim,
                              float((w["nope"] + w["rope"]) ** -0.5), decode_pages, j)
            o_s = jnp.pad(o_s.reshape(bs, nh * vdim), ((0, SAMPLE_PAD - bs), (0, 0)))
            xs = _mm(o_s, w["w_out"], resid=xs)
            lat_s.append(lat1[:bs].reshape(bs, 1, kv_lora))
            kr_s.append(kr1[:bs].reshape(bs, 1, -1))
            ksc_s.append(ksc1[:bs].reshape(bs, 1, nh))
        wq = peer_w_q[i].astype(BF16)
        keys = tuple(_split(peer_sub_keys[i].reshape(peer_heads * 2, n_keys, peer_half), 2))
        u = peer_u[i].astype(BF16)
        vt = peer_v[i].astype(BF16).reshape(-1, PEER_SLICE, d).transpose(0, 2, 1)
        xp = _peer(xp, norm_ffn[i], wq, keys, u, vt, peer_heads, peer_tblk_p)
        xs = _peer(xs, norm_ffn[i], wq, keys, u, vt, peer_heads, SAMPLE_PAD)
    return (xp.reshape(bp, seq, d), xs[:bs].reshape(bs, 1, d),
            jnp.stack(lat_p), jnp.stack(kr_p), jnp.stack(ksc_p), jnp.stack(ssm_p), jnp.stack(conv_p),
            jnp.stack(lat_s), jnp.stack(kr_s), jnp.stack(ksc_s), jnp.stack(ssm_s), jnp.stack(conv_s))
```

```python
import functools
import math

import numpy as np
import jax
import jax.numpy as jnp
from jax import lax
from jax.experimental import pallas as pl
from jax.experimental.pallas import tpu as pltpu

F32 = jnp.float32
BF16 = jnp.bfloat16
HI = lax.Precision.HIGHEST

EPS = 1e-6
ROPE_THETA = 10000.0
GDN_CHUNK = 64
CONV_W = 4
PEER_TOPK = 16
LANES = 128
SUBLANES = 8
VMEM_LIMIT = 52 * 1024 * 1024
SAMPLE_PAD = 128
PEER_SLICE = 256

NT_DIMS = (((1,), (1,)), ((), ()))
TN_DIMS = (((0,), (0,)), ((), ()))


def _params(sem):
    return pltpu.CompilerParams(dimension_semantics=sem, vmem_limit_bytes=VMEM_LIMIT)


def _iota(shape, dim):
    return lax.broadcasted_iota(jnp.int32, shape, dim)


def _aligned(x, m):
    return x if isinstance(x, int) else pl.multiple_of(x, m)


def _silu(x):
    return x * jax.nn.sigmoid(x)


def _softplus(x):
    return jnp.maximum(x, 0.0) + jnp.log1p(jnp.exp(-jnp.abs(x)))


def _dot(a, b, precision=None):
    return jnp.dot(a, b, preferred_element_type=F32, precision=precision)


def _dot_nt(a, b, precision=None):
    return lax.dot_general(a, b, NT_DIMS, preferred_element_type=F32, precision=precision)


def _split(x, terms):
    out = []
    for _ in range(terms):
        t = x.astype(BF16)
        out.append(t)
        x = x - t.astype(F32)
    return out


def _dot3(a, b, dot=_dot):
    return dot(a[0], b[0]) + (dot(a[0], b[1]) + dot(a[1], b[0]))


def _dot_exact01(a3, b01, dot=_dot):
    return dot(a3[0], b01) + (dot(a3[1], b01) + dot(a3[2], b01))


def _mm_kernel(*refs, has_gain, has_resid, emit_xn):
    it = iter(refs)
    x_ref, w_ref = next(it), next(it)
    g_ref = next(it) if has_gain else None
    r_ref = next(it) if has_resid else None
    o_ref = next(it)
    xn_ref = next(it) if emit_xn else None
    xb_scr = next(it)

    @pl.when(pl.program_id(1) == 0)
    def _():
        x = x_ref[...].astype(F32)
        if has_gain:
            x = x * lax.rsqrt(jnp.mean(x * x, axis=-1, keepdims=True) + EPS) * g_ref[...]
        xb = x.astype(BF16)
        xb_scr[...] = xb
        if emit_xn:
            xn_ref[...] = xb

    y = _dot(xb_scr[...], w_ref[...])
    if has_resid:
        y = y + r_ref[...]
    o_ref[...] = y.astype(o_ref.dtype)


def _pick_tile(n, cap):
    if n <= cap:
        return n
    best = LANES
    for t in range(LANES, cap + 1, LANES):
        if n % t == 0:
            best = t
    return best


def _mm(x, w, gain=None, resid=None, emit_xn=False, tm_cap=1024, tn_cap=1024):
    m, k = x.shape
    n = w.shape[1]
    tm = _pick_tile(m, tm_cap)
    tn = _pick_tile(n, tn_cap)
    assert m % tm == 0 and n % tn == 0
    in_specs = [pl.BlockSpec((tm, k), lambda i, j: (i, 0)),
                pl.BlockSpec((k, tn), lambda i, j: (0, j))]
    args = [x, w]
    if gain is not None:
        in_specs.append(pl.BlockSpec((1, k), lambda i, j: (0, 0)))
        args.append(gain.reshape(1, k).astype(F32))
    if resid is not None:
        in_specs.append(pl.BlockSpec((tm, tn), lambda i, j: (i, j)))
        args.append(resid)
    out_shape = [jax.ShapeDtypeStruct((m, n), F32)]
    out_specs = [pl.BlockSpec((tm, tn), lambda i, j: (i, j))]
    if emit_xn:
        out_shape.append(jax.ShapeDtypeStruct((m, k), BF16))
        out_specs.append(pl.BlockSpec((tm, k), lambda i, j: (i, 0)))
    res = pl.pallas_call(
        functools.partial(_mm_kernel, has_gain=gain is not None, has_resid=resid is not None, emit_xn=emit_xn),
        grid=(m // tm, n // tn),
        in_specs=in_specs,
        out_specs=out_specs,
        out_shape=out_shape,
        scratch_shapes=[pltpu.VMEM((tm, k), BF16)],
        compiler_params=_params(("parallel", "arbitrary")),
        name="norm_matmul",
    )(*args)
    return res if emit_xn else res[0]


def _unit_lower_inverse(low, eye, ri, ci, chunk):
    base = min(16, chunk)
    assert chunk & (chunk - 1) == 0
    same = lambda size: (ri >> (size.bit_length() - 1)) == (ci >> (size.bit_length() - 1))
    pw = [-jnp.where(same(base), l, 0.0) for l in low]
    inv = [eye + p for p in pw]
    ps = [_split(p, 2) for p in pw]
    for _ in range(max(int(math.ceil(math.log2(base))) - 1, 0)):
        ps = [_split(_dot3(p, p), 2) for p in ps]
        inv = [t + _dot3(_split(t, 2), p) for t, p in zip(inv, ps)]
    size = base
    while size < chunk:
        off = jnp.logical_and(same(2 * size), jnp.logical_not(same(size)))
        ts = [_split(t, 2) for t in inv]
        x = [_dot3(t, _split(jnp.where(off, l, 0.0), 2)) for t, l in zip(ts, low)]
        x = [_dot3(_split(xi, 2), t) for xi, t in zip(x, ts)]
        inv = [t - xi for t, xi in zip(inv, x)]
        size *= 2
    return inv


def _gdn_kernel(q_ref, k_ref, v_ref, z_ref, rows_ref, alog_ref, dt_ref, cwq_ref, cwk_ref, cwv_ref,
                gout_ref, bq_ref, bk_ref, bv_ref, s0_ref, o_ref, sfin_ref,
                xq_scr, xk_scr, xv_scr, s_scr, *, tb, chunk, hb, valid_len):
    t = pl.program_id(2)
    dk = s_scr.shape[-1]
    hist = CONV_W - 1
    base = SUBLANES - hist
    ncb = tb // chunk

    @pl.when(t == 0)
    def _():
        s_scr[...] = s0_ref[0]
        xq_scr[base:SUBLANES, :] = bq_ref[0]
        xk_scr[base:SUBLANES, :] = bk_ref[0]
        xv_scr[base:SUBLANES, :] = bv_ref[0]

    def conv_silu(x_ref, w_ref, scr):
        scr[SUBLANES:SUBLANES + tb, :] = x_ref[0]
        y = w_ref[0:1, :] * scr[base:base + tb, :]
        for j in range(1, CONV_W):
            y = y + w_ref[j:j + 1, :] * scr[base + j:base + j + tb, :]
        scr[base:SUBLANES, :] = scr[tb + base:tb + SUBLANES, :]
        return _silu(y)

    def l2n(x):
        return x * lax.rsqrt(jnp.sum(x * x, axis=-1, keepdims=True) + EPS)

    q_all = conv_silu(q_ref, cwq_ref, xq_scr)
    k_all = conv_silu(k_ref, cwk_ref, xk_scr)
    v_all = conv_silu(v_ref, cwv_ref, xv_scr)

    ri = _iota((chunk, chunk), 0)
    ci = _iota((chunk, chunk), 1)
    eye = (ri == ci).astype(F32)
    incl = ri >= ci
    strict = ri > ci
    eye_b = eye.astype(BF16)
    tril_b = incl.astype(F32).astype(BF16)
    triu_b = (ri <= ci).astype(F32).astype(BF16)
    rid8 = _iota((SUBLANES, chunk), 0)

    units = [(hh, c) for c in range(ncb) for hh in range(hb)]
    qs, ks, vs, beta, g_col, decay, low = [], [], [], [], [], [], []
    for hh, c in units:
        hs = slice(hh * dk, (hh + 1) * dk)
        r0 = c * chunk
        qs.append(l2n(q_all[r0:r0 + chunk, hs]) * (dk ** -0.5))
        ks.append(l2n(k_all[r0:r0 + chunk, hs]))
        vs.append(v_all[r0:r0 + chunk, hs])
        raw = rows_ref[0, hh, c]
        m8 = jnp.where(rid8 == 0, jax.nn.sigmoid(raw),
                       jnp.where(rid8 == 1, -jnp.exp(alog_ref[hh]) * _softplus(raw + dt_ref[hh]), 0.0))
        if valid_len is not None:
            pos = t * tb + r0 + _iota((SUBLANES, chunk), 1)
            m8 = jnp.where(pos < valid_len, m8, 0.0)
        m3 = _split(m8, 3)
        flip = lambda a, b: _dot_nt(b, a)
        g_row = _dot_exact01(m3, triu_b)[1:2, :]
        beta.append(_dot_exact01(m3, eye_b, flip)[:, 0:1])
        g_col.append(_dot_exact01(m3, tril_b, flip)[:, 1:2])
        decay.append(jnp.exp(jnp.where(incl, g_col[-1] - g_row, -jnp.inf)))
    kb = [x.astype(BF16) for x in ks]
    for n in range(len(units)):
        low.append(jnp.where(strict, beta[n] * _dot_nt(kb[n], kb[n]) * decay[n], 0.0))
    e_g = [jnp.exp(g) for g in g_col]
    rhs = [jnp.concatenate([vs[n] * beta[n], ks[n] * (beta[n] * e_g[n])], axis=1) for n in range(len(units))]
    if valid_len == 1:
        w = rhs
    else:
        inv = _unit_lower_inverse(low, eye, ri, ci, chunk)
        w = [_dot3(_split(inv[n], 2), _split(rhs[n], 2)) for n in range(len(units))]
    qk = [_dot_nt(qs[n].astype(BF16), kb[n]) * decay[n] for n in range(len(units))]

    for n, (hh, c) in enumerate(units):
        hs = slice(hh * dk, (hh + 1) * dk)
        r0 = c * chunk
        g_last = g_col[n][chunk - 1:chunk, :]
        q_dec = qs[n] * e_g[n]
        k_dec = ks[n] * jnp.exp(g_last - g_col[n])
        s = s_scr[hh]
        sb = s.astype(BF16)
        u = w[n][:, :dk] - _dot(w[n][:, dk:].astype(BF16), sb)
        ub = u.astype(BF16)
        o = _dot(q_dec.astype(BF16), sb) + _dot(qk[n].astype(BF16), ub)
        s_scr[hh] = s * jnp.exp(g_last) + lax.dot_general(k_dec.astype(BF16), ub, TN_DIMS,
                                                           preferred_element_type=F32)
        on = o * lax.rsqrt(jnp.mean(o * o, axis=-1, keepdims=True) + EPS) * gout_ref[...]
        o_ref[0, r0:r0 + chunk, hs] = on * _silu(z_ref[0, r0:r0 + chunk, hs])

    @pl.when(t == pl.num_programs(2) - 1)
    def _():
        sfin_ref[0] = s_scr[...]


def _gdn_mixer(proj, rows, a_log, dt_bias, conv_w, g_out, buf0, s0, valid_len, tb, hb):
    b, t, _ = proj.shape
    nh = s0.shape[1]
    dk = s0.shape[2]
    chunk = rows.shape[-1]
    dim = nh * dk
    ng = nh // hb
    wdt = hb * dk
    col = lambda off: pl.BlockSpec((1, tb, wdt), lambda bi, h, ti, off=off: (bi, ti, off + h))
    cw = lambda off: pl.BlockSpec((CONV_W, wdt), lambda bi, h, ti, off=off: (0, off + h))
    bf = lambda off: pl.BlockSpec((1, CONV_W - 1, wdt), lambda bi, h, ti, off=off: (bi, 0, off + h))
    hscal = pl.BlockSpec((hb, 1, 1), lambda bi, h, ti: (h, 0, 0))
    o, sfin = pl.pallas_call(
        functools.partial(_gdn_kernel, tb=tb, chunk=chunk, hb=hb, valid_len=valid_len),
        grid=(b, ng, t // tb),
        in_specs=[col(0), col(ng), col(2 * ng), col(3 * ng),
                  pl.BlockSpec((1, hb, tb // chunk, SUBLANES, chunk), lambda bi, h, ti: (bi, h, ti, 0, 0)),
                  hscal, hscal, cw(0), cw(ng), cw(2 * ng),
                  pl.BlockSpec((1, dk), lambda bi, h, ti: (0, 0)),
                  bf(0), bf(ng), bf(2 * ng),
                  pl.BlockSpec((1, hb, dk, dk), lambda bi, h, ti: (bi, h, 0, 0))],
        out_specs=[pl.BlockSpec((1, tb, wdt), lambda bi, h, ti: (bi, ti, h)),
                   pl.BlockSpec((1, hb, dk, dk), lambda bi, h, ti: (bi, h, 0, 0))],
        out_shape=[jax.ShapeDtypeStruct((b, t, dim), F32),
                   jax.ShapeDtypeStruct((b, nh, dk, dk), F32)],
        scratch_shapes=[pltpu.VMEM((tb + SUBLANES, wdt), F32)] * 3 + [pltpu.VMEM((hb, dk, dk), F32)],
        compiler_params=_params(("parallel", "parallel", "arbitrary")),
        name="gdn_chunked",
    )(proj, proj, proj, proj, rows, a_log.reshape(nh, 1, 1), dt_bias.reshape(nh, 1, 1),
      conv_w, conv_w, conv_w, g_out.reshape(1, dk), buf0, buf0, buf0, s0)
    return o, sfin


def _gdn_rows(ba, nh, chunk):
    b, t, _ = ba.shape
    x = jnp.stack([ba[..., :nh], ba[..., nh:2 * nh]], axis=-1)
    x = x.reshape(b, t // chunk, chunk, nh, 2).transpose(0, 3, 1, 4, 2)
    return jnp.pad(x, ((0, 0),) * 3 + ((0, SUBLANES - 2), (0, 0)))


def _rope_block(x, cos_t, sin_a, sin_b):
    q4 = x.shape[-1] // 4
    return x * cos_t + pltpu.roll(x, 3 * q4, 1) * sin_a + pltpu.roll(x, q4, 1) * sin_b


def _mla_prep_kernel(p_ref, gcq_ref, gckv_ref, wuq_ref, gqn_ref, gqr_ref, wuk_ref, seg_ref, segt_ref,
                     seg8_ref, cos_ref, sina_ref, sinb_ref,
                     lat_ref, kr_ref, ksc_ref, qcat_ref, kcat_ref, kst_ref,
                     *, q_lora, kv_lora, nh, nope, rope_dim, scale):
    qk_dim = nope + rope_dim
    hn = nh * nope
    x = p_ref[...]
    cq = x[:, :q_lora]
    cq = cq * lax.rsqrt(jnp.mean(cq * cq, axis=-1, keepdims=True) + EPS) * gcq_ref[...]
    ckv = x[:, q_lora:q_lora + kv_lora]
    ckv = ckv * lax.rsqrt(jnp.mean(ckv * ckv, axis=-1, keepdims=True) + EPS) * gckv_ref[...]
    cos_t, sin_a, sin_b = cos_ref[...], sina_ref[...], sinb_ref[...]
    krb = _rope_block(x[:, q_lora + kv_lora:], cos_t, sin_a, sin_b)
    lat_ref[...] = ckv
    kr_ref[...] = krb[:, :rope_dim]
    ckv_b = ckv.astype(BF16)
    kcat_ref[:, :kv_lora] = ckv_b
    kcat_ref[:, kv_lora:] = krb.astype(BF16)

    qf = _dot(cq.astype(BF16), wuq_ref[...])
    qn = qf[:, :hn]
    qr = jnp.concatenate([_rope_block(qf[:, hn + h * nope:hn + (h + 1) * nope], cos_t, sin_a, sin_b)
                          for h in range(nh)], axis=1)
    ssq = _dot_exact01(_split(qn * qn + qr * qr, 3), seg_ref[...])
    inv = lax.rsqrt(ssq * (1.0 / qk_dim) + EPS)
    inv_x = _dot_exact01(_split(inv, 3), segt_ref[...])
    qn = qn * inv_x * gqn_ref[...]
    qr = qr * inv_x * gqr_ref[...]

    k_nope = _dot(ckv_b, wuk_ref[...])
    k2 = k_nope * k_nope
    kr2 = krb * krb
    k2s = _split(k2, 3)
    ksq = _dot_exact01(k2s, seg_ref[...]) + jnp.sum(kr2, axis=-1, keepdims=True)
    ksc = lax.rsqrt(ksq * (1.0 / qk_dim) + EPS)
    ksc_ref[...] = ksc[:, :nh]
    flip = lambda a, b: _dot_nt(b, a)
    ksq_t = (_dot_exact01(k2s, seg8_ref[...], flip)
             + _dot_exact01(_split(kr2, 3), jnp.ones((SUBLANES, kr2.shape[1]), BF16), flip))
    kst_ref[...] = lax.rsqrt(ksq_t * (1.0 / qk_dim) + EPS) * scale

    for h in range(nh):
        sl = slice(h * nope, (h + 1) * nope)
        q_lat = _dot_nt(qn[:, sl].astype(BF16), wuk_ref[:, sl])
        qcat_ref[h, :, :kv_lora] = q_lat.astype(BF16)
        qcat_ref[h, :, kv_lora:] = qr[:, sl].astype(BF16)


def _mla_prep(proj, w, cos_t, sin_a, sin_b, tm):
    m = proj.shape[0]
    nh, nope, rope_dim, q_lora, kv_lora = w["nh"], w["nope"], w["rope"], w["q_lora"], w["kv_lora"]
    kc = kv_lora + LANES
    full = lambda a: pl.BlockSpec(a.shape, lambda i: (0,) * a.ndim)
    rows = lambda n: pl.BlockSpec((tm, n), lambda i: (i, 0))
    consts = [w["gcq"], w["gckv"], w["wuq"], w["gqn"], w["gqr"], w["wuk"], w["seg"], w["segt"], w["seg8"]]
    return pl.pallas_call(
        functools.partial(_mla_prep_kernel, q_lora=q_lora, kv_lora=kv_lora, nh=nh, nope=nope,
                          rope_dim=rope_dim, scale=float((nope + rope_dim) ** -0.5)),
        grid=(m // tm,),
        in_specs=[rows(proj.shape[1])] + [full(a) for a in consts] + [rows(LANES)] * 3,
        out_specs=[rows(kv_lora), rows(rope_dim), rows(nh),
                   pl.BlockSpec((nh, tm, kc), lambda i: (0, i, 0)), rows(kc),
                   pl.BlockSpec((SUBLANES, tm), lambda i: (0, i))],
        out_shape=[jax.ShapeDtypeStruct((m, kv_lora), F32), jax.ShapeDtypeStruct((m, rope_dim), F32),
                   jax.ShapeDtypeStruct((m, nh), F32), jax.ShapeDtypeStruct((nh, m, kc), BF16),
                   jax.ShapeDtypeStruct((m, kc), BF16), jax.ShapeDtypeStruct((SUBLANES, m), F32)],
        compiler_params=_params(("parallel",)),
        name="mla_prep",
    )(proj, *consts, cos_t, sin_a, sin_b)


def _attn_kernel(q_ref, k_ref, ks_ref, wuv_ref, o_ref, m_scr, l_scr, acc_scr, *, nh, kv_lora, vdim, tq, tk):
    qi, kj = pl.program_id(1), pl.program_id(2)

    @pl.when(kj == 0)
    def _():
        m_scr[...] = jnp.full(m_scr.shape, -jnp.inf, F32)
        l_scr[...] = jnp.zeros(l_scr.shape, F32)
        acc_scr[...] = jnp.zeros(acc_scr.shape, F32)

    def update(diagonal):
        kblk = k_ref[0]
        vblk = kblk[:, :kv_lora]
        if diagonal:
            causal = _iota((tq, tk), 1) <= _iota((tq, tk), 0)
        for h in range(nh):
            s = _dot_nt(q_ref[h, 0], kblk) * ks_ref[0, h:h + 1, :]
            if diagonal:
                s = jnp.where(causal, s, -jnp.inf)
            m_old = m_scr[h]
            m_new = jnp.maximum(m_old, jnp.max(s, axis=-1, keepdims=True))
            alpha = jnp.exp(m_old - m_new)
            p = jnp.exp(s - m_new)
            l_scr[h] = alpha * l_scr[h] + jnp.sum(p, axis=-1, keepdims=True)
            acc_scr[h] = alpha * acc_scr[h] + _dot(p.astype(BF16), vblk)
            m_scr[h] = m_new

    @pl.when(kj < qi)
    def _():
        update(False)

    @pl.when(kj == qi)
    def _():
        update(True)
        for h in range(nh):
            o_lat = acc_scr[h] / l_scr[h]
            o_ref[0, :, h * vdim:(h + 1) * vdim] = _dot(o_lat.astype(BF16), wuv_ref[:, h * vdim:(h + 1) * vdim])


def _mla_attention(qcat, kcat, kst, wuv, nh, kv_lora, vdim, tq):
    _, b, t, kc = qcat.shape
    tk = tq
    return pl.pallas_call(
        functools.partial(_attn_kernel, nh=nh, kv_lora=kv_lora, vdim=vdim, tq=tq, tk=tk),
        grid=(b, t // tq, t // tk),
        in_specs=[pl.BlockSpec((nh, 1, tq, kc), lambda bi, qi, kj: (0, bi, qi, 0)),
                  pl.BlockSpec((1, tk, kc), lambda bi, qi, kj: (bi, jnp.minimum(kj, qi), 0)),
                  pl.BlockSpec((1, SUBLANES, tk), lambda bi, qi, kj: (bi, 0, jnp.minimum(kj, qi))),
                  pl.BlockSpec(wuv.shape, lambda bi, qi, kj: (0, 0))],
        out_specs=pl.BlockSpec((1, tq, nh * vdim), lambda bi, qi, kj: (bi, qi, 0)),
        out_shape=jax.ShapeDtypeStruct((b, t, nh * vdim), F32),
        scratch_shapes=[pltpu.VMEM((nh, tq, 1), F32), pltpu.VMEM((nh, tq, 1), F32),
                        pltpu.VMEM((nh, tq, kv_lora), F32)],
        compiler_params=_params(("parallel", "parallel", "arbitrary")),
        name="mla_prefill_attention",
    )(qcat, kcat, kst, wuv)


def _decode_kernel(pt_ref, q_ref, *refs, nh, kv_lora, rope_dim, vdim, scale, npg):
    lat_refs, kr_refs, ks_refs = refs[:npg], refs[npg:2 * npg], refs[2 * npg:3 * npg]
    kself_ref, ksself_ref, wuv_ref, o_ref, m_scr, l_scr, acc_scr = refs[3 * npg:]
    p = pl.program_id(1)

    @pl.when(p == 0)
    def _():
        m_scr[...] = jnp.full(m_scr.shape, -jnp.inf, F32)
        l_scr[...] = jnp.zeros(l_scr.shape, F32)
        acc_scr[...] = jnp.zeros(acc_scr.shape, F32)

    q = q_ref[0]
    q_lat, q_rope = q[:, :kv_lora], q[:, kv_lora:kv_lora + rope_dim]
    lat = [r[0, 0].astype(BF16) for r in lat_refs]
    s = jnp.concatenate(
        [(_dot_nt(q_lat, lat[g]) + _dot_nt(q_rope, kr_refs[g][0, 0].astype(BF16))) * (ks_refs[g][0] * scale)
         for g in range(npg)], axis=1)
    m_old = m_scr[...]
    m_new = jnp.maximum(m_old, jnp.max(s, axis=-1, keepdims=True))
    alpha = jnp.exp(m_old - m_new)
    pr = jnp.exp(s - m_new)
    l_scr[...] = alpha * l_scr[...] + jnp.sum(pr, axis=-1, keepdims=True)
    page = lat[0].shape[0]
    pv = _dot(pr[:, :page].astype(BF16), lat[0])
    for g in range(1, npg):
        pv = pv + _dot(pr[:, g * page:(g + 1) * page].astype(BF16), lat[g])
    acc_scr[...] = alpha * acc_scr[...] + pv
    m_scr[...] = m_new

    @pl.when(p == pl.num_programs(1) - 1)
    def _():
        kself = kself_ref[0].astype(F32)
        s_self = jnp.sum(q.astype(F32) * kself, axis=-1, keepdims=True) * ksself_ref[0]
        m_old = m_scr[...]
        m_new = jnp.maximum(m_old, s_self)
        alpha = jnp.exp(m_old - m_new)
        p_self = jnp.exp(s_self - m_new)
        l_fin = alpha * l_scr[...] + p_self
        acc = alpha * acc_scr[...] + p_self.astype(BF16).astype(F32) * kself[:, :kv_lora]
        o_all = _dot((acc / l_fin).astype(BF16), wuv_ref[...])
        for h in range(nh):
            o_ref[0, :, h * vdim:(h + 1) * vdim] = o_all[h:h + 1, h * vdim:(h + 1) * vdim]


def _mla_decode(page_table, qd, cache_lat, cache_kr, cache_kst, kself, ksself, wuv, nh, kv_lora, rope_dim, vdim,
                scale, npg, layer):
    b, n_pages = page_table.shape
    page = cache_lat.shape[2]
    kc = qd.shape[-1]
    assert n_pages % npg == 0
    pg = lambda shape, g: pl.BlockSpec(shape, lambda bi, p, pt, g=g: (layer, pt[bi, p * npg + g], 0, 0))
    grid_spec = pltpu.PrefetchScalarGridSpec(
        num_scalar_prefetch=1,
        grid=(b, n_pages // npg),
        in_specs=([pl.BlockSpec((1, nh, kc), lambda bi, p, pt: (bi, 0, 0))]
                  + [pg((1, 1, page, kv_lora), g) for g in range(npg)]
                  + [pg((1, 1, page, rope_dim), g) for g in range(npg)]
                  + [pl.BlockSpec((1, nh, page), lambda bi, p, pt, g=g: (pt[bi, p * npg + g], 0, 0))
                     for g in range(npg)]
                  + [pl.BlockSpec((1, 1, kc), lambda bi, p, pt: (bi, 0, 0)),
                     pl.BlockSpec((1, nh, 1), lambda bi, p, pt: (bi, 0, 0)),
                     pl.BlockSpec(wuv.shape, lambda bi, p, pt: (0, 0))]),
        out_specs=pl.BlockSpec((1, 1, nh * vdim), lambda bi, p, pt: (bi, 0, 0)),
        scratch_shapes=[pltpu.VMEM((nh, 1), F32), pltpu.VMEM((nh, 1), F32), pltpu.VMEM((nh, kv_lora), F32)],
    )
    return pl.pallas_call(
        functools.partial(_decode_kernel, nh=nh, kv_lora=kv_lora, rope_dim=rope_dim, vdim=vdim, scale=scale,
                          npg=npg),
        grid_spec=grid_spec,
        out_shape=jax.ShapeDtypeStruct((b, 1, nh * vdim), F32),
        compiler_params=_params(("parallel", "arbitrary")),
        name="mla_paged_decode",
    )(page_table, qd, *([cache_lat] * npg), *([cache_kr] * npg), *([cache_kst] * npg), kself, ksself, wuv)


def _oddeven_merge_sort_pairs(n):
    pairs = []

    def merge(lo, m, r):
        step = r * 2
        if step < m:
            merge(lo, m, step)
            merge(lo + r, m, step)
            pairs.extend((i, i + r) for i in range(lo + r, lo + m - r, step))
        else:
            pairs.append((lo, lo + r))

    def sort(lo, m):
        if m > 1:
            sort(lo, m // 2)
            sort(lo + m // 2, m // 2)
            merge(lo, m, 1)

    sort(0, n)
    return pairs


def _cmp_exchange(v, i, j):
    a, b = v[i], v[j]
    if b is None:
        return
    if a is None:
        v[i], v[j] = b, None
        return
    v[i], v[j] = jnp.maximum(a, b), jnp.minimum(a, b)


def _top_sorted(v):
    k = len(v)
    v = list(v)
    for i, j in _oddeven_merge_sort_pairs(k):
        _cmp_exchange(v, i, j)
    shift = SUBLANES // 2
    while shift >= 1:
        rot = [None if x is None else pltpu.roll(x, shift, 0) for x in v]
        v = [rot[k - 1 - i] if v[i] is None else (v[i] if rot[k - 1 - i] is None
                                                   else jnp.maximum(v[i], rot[k - 1 - i])) for i in range(k)]
        d = k // 2
        while d >= 1:
            for i in range(k):
                if i & d == 0:
                    _cmp_exchange(v, i, i + d)
            d //= 2
        shift //= 2
    return v


def _peer_select_kernel(q_ref, khi_ref, klo_ref, s0_ref, s1_ref, e0_ref, e1_ref, thr_ref, *, nh, half, topk):
    tb = q_ref.shape[0]
    n_keys = khi_ref.shape[1]
    sub = _iota((SUBLANES, tb), 0)
    hid = _iota((nh, tb), 0)
    neg = jnp.full((SUBLANES, tb), -jnp.inf, F32)

    def spread(vals):
        out = vals[SUBLANES - 1]
        for r in range(SUBLANES - 2, -1, -1):
            out = jnp.where(sub == r, vals[r], out)
        return out

    thr_all = jnp.zeros((nh, tb), F32)
    for h in range(nh):
        s, sv = [], []
        for p in range(2):
            hp = 2 * h + p
            sp = _dot3((khi_ref[hp], klo_ref[hp]), _split(q_ref[:, hp * half:(hp + 1) * half], 2),
                       dot=_dot_nt)
            s.append(sp)
            tiles = [sp[r * SUBLANES:(r + 1) * SUBLANES, :] for r in range(n_keys // SUBLANES)]
            sv.append(_top_sorted(tiles)[:topk])
        a, b = sv
        b_lo, b_hi = spread(b[:SUBLANES]), spread(b[SUBLANES:])
        a_lo = jnp.where(sub >= 4, spread(a[:SUBLANES]), neg)
        a_hi = spread(a[SUBLANES:])
        cand = [a[0] + b_lo, a[0] + b_hi, a[1] + b_lo, a[2] + b_lo, a[3] + b_lo,
                a_lo + b[0], a_hi + b[0], a_lo + b[1], a_lo + b[2]]
        ts = _top_sorted(cand + [None] * (topk - len(cand)))
        mx = ts[0][0:1, :]
        tot = jnp.exp(ts[0][0:1, :] - mx)
        for r in range(1, topk):
            tot = tot + jnp.exp(ts[r][0:1, :] - mx)
        lse = mx + jnp.log(tot)
        m0 = a[0][0:1, :]
        s0_ref[h] = s[0]
        s1_ref[h] = s[1]
        e0_ref[h] = jnp.exp(s[0] - m0)
        e1_ref[h] = jnp.exp(s[1] - (lse - m0))
        thr_all = jnp.where(hid == h, ts[topk - 1][0:1, :], thr_all)
    thr_ref[...] = thr_all


def _peer_select(q, keys, nh, tb):
    t = q.shape[0]
    n_keys, half = keys[0].shape[1], keys[0].shape[2]
    assert PEER_TOPK == 16 and n_keys == 16 * SUBLANES
    big = pl.BlockSpec((nh, n_keys, tb), lambda i: (0, 0, i))
    keys_hi, keys_lo = keys
    return pl.pallas_call(
        functools.partial(_peer_select_kernel, nh=nh, half=half, topk=PEER_TOPK),
        grid=(t // tb,),
        in_specs=[pl.BlockSpec((tb, q.shape[1]), lambda i: (i, 0)),
                  pl.BlockSpec(keys_hi.shape, lambda i: (0, 0, 0)),
                  pl.BlockSpec(keys_lo.shape, lambda i: (0, 0, 0))],
        out_specs=[big, big, big, big, pl.BlockSpec((nh, tb), lambda i: (0, i))],
        out_shape=[jax.ShapeDtypeStruct((nh, n_keys, t), F32)] * 4 + [jax.ShapeDtypeStruct((nh, t), F32)],
        compiler_params=_params(("parallel",)),
        name="peer_select",
    )(q, keys_hi, keys_lo)


def _peer_dense_kernel(xn_ref, u_ref, unext_ref, vt_ref, vtprev_ref, s0_ref, s1_ref, e0_ref, e1_ref, thr_ref,
                       res_ref, o_ref, ht_scr, ht2_scr, a_scr, a2_scr, acc_scr, s0x_scr, e0x_scr,
                       *, nh, n_keys, e_blk, t_blk, rows_per_slice):
    e = pl.program_id(1)
    n_i = e_blk // n_keys
    n_sl = n_i // rows_per_slice
    e_sl = rows_per_slice * n_keys
    sqrt_half = float(np.sqrt(0.5))
    ht_bufs, a_bufs = (ht_scr, ht2_scr), (a_scr, a2_scr)
    last = (n_sl - 1) % 2

    @pl.when(e == 0)
    def _():
        acc_scr[...] = jnp.zeros(acc_scr.shape, F32)
        a_bufs[last][...] = jnp.zeros(a_bufs[last].shape, BF16)
        ht_bufs[0][...] = _dot_nt(xn_ref[...], u_ref[0:e_sl, :]).T

    i0 = pl.multiple_of(e * n_i, n_i)
    for h in range(nh):
        srows = s0_ref[h, pl.ds(i0, n_i), :]
        erows = e0_ref[h, pl.ds(i0, n_i), :]
        for ii in range(n_i):
            s0x_scr[ii, h:h + 1, :] = srows[ii:ii + 1, :]
            e0x_scr[ii, h:h + 1, :] = erows[ii:ii + 1, :]

    def pre_activations(m):
        u_sl = unext_ref[...] if m == n_sl else u_ref[m * e_sl:(m + 1) * e_sl, :]
        ht_bufs[m % 2][...] = _dot_nt(xn_ref[...], u_sl).T

    def gated_activations(m):
        nv = n_keys // SUBLANES
        for r in range(rows_per_slice):
            ii = m * rows_per_slice + r
            for tc in range(t_blk // LANES):
                ls = slice(tc * LANES, (tc + 1) * LANES)
                gate = [None] * nv
                for h in range(nh):
                    s0r = jnp.broadcast_to(s0x_scr[ii, h:h + 1, ls], (SUBLANES, LANES))
                    e0r = jnp.broadcast_to(e0x_scr[ii, h:h + 1, ls], (SUBLANES, LANES))
                    thr = jnp.broadcast_to(thr_ref[h:h + 1, ls], (SUBLANES, LANES))
                    for v in range(nv):
                        ks = slice(v * SUBLANES, (v + 1) * SUBLANES)
                        ssum = s1_ref[h, ks, ls] + s0r
                        term = jnp.where(ssum >= thr, e1_ref[h, ks, ls], 0.0) * e0r
                        gate[v] = term if gate[v] is None else gate[v] + term
                for v in range(0, nv, 2):
                    rs = slice(r * n_keys + v * SUBLANES, r * n_keys + (v + 2) * SUBLANES)
                    hv = ht_bufs[m % 2][rs, ls]
                    act = 0.5 * hv * (1.0 + lax.erf(hv * sqrt_half))
                    a_bufs[m % 2][rs, ls] = (act * jnp.concatenate(gate[v:v + 2], axis=0)).astype(BF16)

    def accumulate(m):
        v_sl = vtprev_ref[0] if m == -1 else vt_ref[m]
        acc_scr[...] += _dot(v_sl, a_bufs[m % 2][...])

    assert n_sl % 2 == 0
    for m in range(n_sl):
        accumulate(m - 1)
        pre_activations(m + 1)
        gated_activations(m)

    @pl.when(e == pl.num_programs(1) - 1)
    def _():
        accumulate(n_sl - 1)
        o_ref[...] = res_ref[...] + acc_scr[...].T


def _peer_dense(xn, u, vt, s0t, s1t, e0t, e1t, thr, resid, t_blk, e_blk):
    t, d = xn.shape
    n_exp = u.shape[0]
    nh, n_keys, _ = s0t.shape
    assert e_blk == SUBLANES * n_keys and nh <= SUBLANES
    e_sl = vt.shape[2]
    n_sl = e_blk // e_sl
    ne = n_exp // e_blk
    big = pl.BlockSpec((nh, n_keys, t_blk), lambda ti, e: (0, 0, ti))
    return pl.pallas_call(
        functools.partial(_peer_dense_kernel, nh=nh, n_keys=n_keys, e_blk=e_blk, t_blk=t_blk,
                          rows_per_slice=e_sl // n_keys),
        grid=(t // t_blk, ne),
        in_specs=[pl.BlockSpec((t_blk, d), lambda ti, e: (ti, 0)),
                  pl.BlockSpec((e_blk, d), lambda ti, e: (e, 0)),
                  pl.BlockSpec((e_sl, d), lambda ti, e: (jnp.minimum(e + 1, ne - 1) * n_sl, 0)),
                  pl.BlockSpec((n_sl, d, e_sl), lambda ti, e: (e, 0, 0)),
                  pl.BlockSpec((1, d, e_sl), lambda ti, e: (jnp.maximum(e * n_sl - 1, 0), 0, 0)),
                  big, big, big, big,
                  pl.BlockSpec((nh, t_blk), lambda ti, e: (0, ti)),
                  pl.BlockSpec((t_blk, d), lambda ti, e: (ti, 0))],
        out_specs=pl.BlockSpec((t_blk, d), lambda ti, e: (ti, 0)),
        out_shape=jax.ShapeDtypeStruct((t, d), F32),
        scratch_shapes=[pltpu.VMEM((e_sl, t_blk), F32), pltpu.VMEM((e_sl, t_blk), F32),
                        pltpu.VMEM((e_sl, t_blk), BF16), pltpu.VMEM((e_sl, t_blk), BF16),
                        pltpu.VMEM((d, t_blk), F32),
                        pltpu.VMEM((e_blk // n_keys, SUBLANES, t_blk), F32),
                        pltpu.VMEM((e_blk // n_keys, SUBLANES, t_blk), F32)],
        compiler_params=_params(("parallel", "arbitrary")),
        name="peer_dense",
    )(xn, u, u, vt, vt, s0t, s1t, e0t, e1t, thr, resid)


def _peer(x, gain, wq, keys, u, vt, nh, t_blk):
    q, xn = _mm(x, wq, gain=gain, emit_xn=True)
    s0t, s1t, e0t, e1t, thr = _peer_select(q, keys, nh, LANES)
    return _peer_dense(xn, u, vt, s0t, s1t, e0t, e1t, thr, x, t_blk, SUBLANES * keys[0].shape[1])


def _rope_tables(pos, rope_dim):
    half = rope_dim // 2
    freq = ROPE_THETA ** (-jnp.arange(half, dtype=F32) / half)
    ang = pos.astype(F32)[:, None] * freq[None, :]
    cos, sin = jnp.cos(ang), jnp.sin(ang)
    z = jnp.zeros_like(cos)
    pad = jnp.zeros((pos.shape[0], LANES - 4 * half), F32)
    cat = lambda *xs: jnp.concatenate(xs + (pad,), axis=1) if pad.shape[1] else jnp.concatenate(xs, axis=1)
    return cat(cos, cos, z, z), cat(-sin, z, z, z), cat(z, sin, z, z)


def _mla_weights(w_in, g_cq, g_ckv, w_uq, g_qn, g_kn, w_uk, w_uv, w_out):
    d = w_in.shape[0]
    q_lora, kv_lora = g_cq.shape[0], g_ckv.shape[0]
    rope_dim = w_in.shape[1] - q_lora - kv_lora
    _, nh, nope = w_uk.shape
    vdim = w_uv.shape[2]
    qk_dim = nope + rope_dim
    assert nope == LANES and 2 * rope_dim <= LANES and rope_dim % 2 == 0 and nh <= SUBLANES
    w_in_p = jnp.concatenate([w_in, jnp.zeros((d, LANES - rope_dim), F32)], axis=1).astype(BF16)
    wq3 = w_uq.reshape(q_lora, nh, qk_dim)
    w_nope = wq3[:, :, :nope].reshape(q_lora, nh * nope)
    w_rope = jnp.pad(wq3[:, :, nope:], ((0, 0), (0, 0), (0, nope - rope_dim))).reshape(q_lora, nh * nope)
    gg = g_qn * g_kn
    hn = nh * nope
    head_of = np.arange(hn) // nope
    seg = np.zeros((hn, LANES), np.float32)
    seg[np.arange(hn), head_of] = 1.0
    seg8 = np.zeros((SUBLANES, hn), np.float32)
    seg8[head_of, np.arange(hn)] = 1.0
    return dict(
        nh=nh, nope=nope, rope=rope_dim, q_lora=q_lora, kv_lora=kv_lora, vdim=vdim,
        w_in=w_in_p,
        gcq=g_cq.reshape(1, -1), gckv=g_ckv.reshape(1, -1),
        wuq=jnp.concatenate([w_nope, w_rope], axis=1).astype(BF16),
        gqn=jnp.tile(gg[:nope], nh).reshape(1, hn),
        gqr=jnp.tile(jnp.pad(gg[nope:], (0, nope - rope_dim)), nh).reshape(1, hn),
        wuk=w_uk.reshape(kv_lora, hn).astype(BF16),
        wuv=w_uv.reshape(kv_lora, nh * vdim).astype(BF16),
        w_out=w_out.astype(BF16),
        seg=jnp.asarray(seg, BF16), segt=jnp.asarray(seg.T.copy(), BF16), seg8=jnp.asarray(seg8, BF16),
    )


def kernel(x_prompt, x_sample, cache_latent, cache_krope, cache_kscale, page_table, state_ssm, state_conv,
           norm_mix, norm_ffn, gdn_w_in, gdn_conv_w, gdn_a_log, gdn_dt_bias, gdn_out_norm, gdn_w_out,
           mla_w_in, mla_cq_norm, mla_ckv_norm, mla_w_uq, mla_q_norm, mla_k_norm, mla_w_uk, mla_w_uv, mla_w_out,
           peer_w_q, peer_sub_keys, peer_u, peer_v):
    bp, seq, d = x_prompt.shape
    bs, dec_seq, _ = x_sample.shape
    assert dec_seq == 1 and bs <= SAMPLE_PAD
    depth = norm_mix.shape[0]
    n_pages, page = page_table.shape[1], cache_latent.shape[2]
    past_len = n_pages * page
    gdn_heads, gdn_dk = state_ssm.shape[2], state_ssm.shape[3]
    gdn_dim = gdn_heads * gdn_dk
    peer_heads, n_keys, peer_half = peer_sub_keys.shape[1], peer_sub_keys.shape[3], peer_sub_keys.shape[4]
    assert n_keys == LANES and peer_half == LANES and gdn_dk == LANES

    tp = bp * seq
    xp = x_prompt.reshape(tp, d)
    xs = jnp.pad(x_sample.reshape(bs, d), ((0, SAMPLE_PAD - bs), (0, 0)))
    gdn_tb = min(512, seq)
    attn_tq = min(512, seq)
    peer_tblk_p = min(512, tp)
    decode_pages = math.gcd(n_pages, 16)

    rope_dim = mla_w_in.shape[2] - mla_cq_norm.shape[1] - mla_ckv_norm.shape[1]
    tabs_p = _rope_tables(jnp.tile(jnp.arange(seq), bp), rope_dim)
    tabs_s = _rope_tables(jnp.full((SAMPLE_PAD,), past_len), rope_dim)

    lat_p, kr_p, ksc_p, ssm_p, conv_p = [], [], [], [], []
    lat_s, kr_s, ksc_s, ssm_s, conv_s = [], [], [], [], []
    for i in range(depth):
        j = i // 2
        if i % 2 == 0:
            w_in = gdn_w_in[j]
            n_extra = w_in.shape[1] - 4 * gdn_dim
            w_in_p = jnp.concatenate([w_in, jnp.zeros((d, LANES - n_extra), F32)], axis=1).astype(BF16)
            w_out = gdn_w_out[j].astype(BF16)
            proj = _mm(xp, w_in_p, gain=norm_mix[i], tn_cap=1408).reshape(bp, seq, -1)
            rows = _gdn_rows(proj[..., 4 * gdn_dim:], gdn_heads, GDN_CHUNK)
            o, sfin = _gdn_mixer(proj, rows, gdn_a_log[j], gdn_dt_bias[j], gdn_conv_w[j], gdn_out_norm[j],
                                 jnp.zeros((bp, CONV_W - 1, 3 * gdn_dim), F32),
                                 jnp.zeros((bp, gdn_heads, gdn_dk, gdn_dk), F32), None, gdn_tb,
                                 math.gcd(gdn_heads, 2))
            xp = _mm(o.reshape(tp, gdn_dim), w_out, resid=xp)
            ssm_p.append(sfin)
            conv_p.append(proj[:, seq - (CONV_W - 1):, :3 * gdn_dim])
            proj_s = _mm(xs, w_in_p, gain=norm_mix[i], tn_cap=1408)[:bs]
            proj_c = jnp.pad(proj_s[:, None, :], ((0, 0), (0, GDN_CHUNK - 1), (0, 0)))
            rows_s = _gdn_rows(proj_c[..., 4 * gdn_dim:], gdn_heads, GDN_CHUNK)
            o_s, sfin_s = _gdn_mixer(proj_c, rows_s, gdn_a_log[j], gdn_dt_bias[j], gdn_conv_w[j], gdn_out_norm[j],
                                     state_conv[j], state_ssm[j], 1, GDN_CHUNK, gdn_heads)
            o_s = jnp.pad(o_s[:, 0, :], ((0, SAMPLE_PAD - bs), (0, 0)))
            xs = _mm(o_s, w_out, resid=xs)
            ssm_s.append(sfin_s)
            conv_s.append(jnp.concatenate([state_conv[j][:, 1:, :], proj_s[:, None, :3 * gdn_dim]], axis=1))
        else:
            w = _mla_weights(mla_w_in[j], mla_cq_norm[j], mla_ckv_norm[j], mla_w_uq[j], mla_q_norm[j],
                             mla_k_norm[j], mla_w_uk[j], mla_w_uv[j], mla_w_out[j])
            nh, kv_lora, vdim = w["nh"], w["kv_lora"], w["vdim"]
            kc = kv_lora + LANES
            proj = _mm(xp, w["w_in"], gain=norm_mix[i])
            lat, kr, ksc, qcat, kcat, kst = _mla_prep(proj, w, *tabs_p, tm=min(256, tp))
            kst_b = kst.reshape(SUBLANES, bp, seq).transpose(1, 0, 2)
            o = _mla_attention(qcat.reshape(nh, bp, seq, kc), kcat.reshape(bp, seq, kc), kst_b, w["wuv"],
                               nh, kv_lora, vdim, attn_tq)
            xp = _mm(o.reshape(tp, nh * vdim), w["w_out"], resid=xp)
            lat_p.append(lat.reshape(bp, seq, kv_lora))
            kr_p.append(kr.reshape(bp, seq, -1))
            ksc_p.append(ksc.reshape(bp, seq, nh))
            proj_s = _mm(xs, w["w_in"], gain=norm_mix[i])
            lat1, kr1, ksc1, qcat1, kcat1, kst1 = _mla_prep(proj_s, w, *tabs_s, tm=SAMPLE_PAD)
            qd = qcat1[:, :bs, :].transpose(1, 0, 2)
            ksself = kst1[:nh, :bs].T.reshape(bs, nh, 1)
            o_s = _mla_decode(page_table, qd, cache_latent, cache_krope, cache_kscale[j].transpose(0, 2, 1),
                              kcat1[:bs].reshape(bs, 1, kc), ksself, w["wuv"], nh, kv_lora, w["rope"], vdim,
                              float((w["nope"] + w["rope"]) ** -0.5), decode_pages, j)
            o_s = jnp.pad(o_s.reshape(bs, nh * vdim), ((0, SAMPLE_PAD - bs), (0, 0)))
            xs = _mm(o_s, w["w_out"], resid=xs)
            lat_s.append(lat1[:bs].reshape(bs, 1, kv_lora))
            kr_s.append(kr1[:bs].reshape(bs, 1, -1))
            ksc_s.append(ksc1[:bs].reshape(bs, 1, nh))
        wq = peer_w_q[i].astype(BF16)
        keys = tuple(_split(peer_sub_keys[i].reshape(peer_heads * 2, n_keys, peer_half), 2))
        u = peer_u[i].astype(BF16)
        vt = peer_v[i].astype(BF16).reshape(-1, PEER_SLICE, d).transpose(0, 2, 1)
        xp = _peer(xp, norm_ffn[i], wq, keys, u, vt, peer_heads, peer_tblk_p)
        xs = _peer(xs, norm_ffn[i], wq, keys, u, vt, peer_heads, SAMPLE_PAD)
    return (xp.reshape(bp, seq, d), xs[:bs].reshape(bs, 1, d),
            jnp.stack(lat_p), jnp.stack(kr_p), jnp.stack(ksc_p), jnp.stack(ssm_p), jnp.stack(conv_p),
            jnp.stack(lat_s), jnp.stack(kr_s), jnp.stack(ksc_s), jnp.stack(ssm_s), jnp.stack(conv_s))
```
